```python
import jax, jax.numpy as jnp
from jax import lax
import numpy as np

D_MODEL = 1024
BATCH = 32
SEQ = 256
DEPTH = 1
DEC_BATCH = 8
DEC_SEQ = 4096
PAST_LEN = 512

GRID_W = 64
D_RNN = 1024
RNN_HEADS = 16
RNN_HEAD_DIM = D_RNN // RNN_HEADS
CONV_W = 4
CONV_LEFT = 2
LRU_C = 8.0
D_GMLP = 1024
GMLP_GROUPS = 8
GMLP_GROUP_DIM = D_GMLP // GMLP_GROUPS
CHUNK = 128
D_FF = 2816
N_MOD = 9
EPS = 1e-6
IN_COLS = 2 * D_RNN + 2 * D_GMLP + 2 * D_MODEL
IN_SPLITS = (D_RNN, 2 * D_RNN, 2 * D_RNN + D_GMLP, 2 * D_RNN + 2 * D_GMLP,
             2 * D_RNN + 2 * D_GMLP + D_MODEL)

kernel_name = "hybrid_rglru_gmlp_diffusion_step"


def rms_norm(x, g):
    xf = x.astype(jnp.float32)
    y = xf * lax.rsqrt(jnp.mean(xf * xf, axis=-1, keepdims=True) + EPS)
    return (y * g.astype(jnp.float32)).astype(x.dtype)


def modulate(x, g, shift, scale):
    return rms_norm(x, g) * (1 + scale) + shift


def grid_pos_embed(n_tokens, dtype):
    rows = n_tokens // GRID_W
    t = jnp.arange(rows * GRID_W)
    r = (t // GRID_W).astype(jnp.float32)
    col = (t % GRID_W).astype(jnp.float32)
    q = D_MODEL // 4
    freqs = 1.0 / (10000.0 ** (jnp.arange(q, dtype=jnp.float32) / q))
    ang_r = r[:, None] * freqs
    ang_c = col[:, None] * freqs
    pe = jnp.concatenate([jnp.sin(ang_r), jnp.cos(ang_r), jnp.sin(ang_c), jnp.cos(ang_c)], axis=-1)
    return pe.astype(dtype)


def swiglu(x, w_gate, w_up, w_down):
    return (jax.nn.silu(x @ w_gate) * (x @ w_up)) @ w_down


def centred_dwconv(x, w, b):
    T = x.shape[1]
    xp = jnp.pad(x, ((0, 0), (CONV_LEFT, CONV_W - 1 - CONV_LEFT), (0, 0)))
    out = xp[:, 0:T] * w[0]
    for k in range(1, CONV_W):
        out = out + xp[:, k:k + T] * w[k]
    return out + b


def _lru_combine(lhs, rhs):
    a1, b1 = lhs
    a2, b2 = rhs
    return a1 * a2, a2 * b1 + b2


def rg_lru(x, h0, w_r, b_r, w_i, b_i, lam, reverse):
    B, T, _ = x.shape
    xh = x.reshape(B, T, RNN_HEADS, RNN_HEAD_DIM)
    r = jax.nn.sigmoid(jnp.einsum('bthi,hij->bthj', xh, w_r).reshape(B, T, D_RNN) + b_r)
    i = jax.nn.sigmoid(jnp.einsum('bthi,hij->bthj', xh, w_i).reshape(B, T, D_RNN) + b_i)
    log_a = (LRU_C * r.astype(jnp.float32)) * jax.nn.log_sigmoid(lam.astype(jnp.float32))
    a = jnp.exp(log_a)
    mult = jnp.sqrt(jnp.maximum(-jnp.expm1(2.0 * log_a), 0.0))
    bx = mult * (i * x).astype(jnp.float32)
    a_cum, b_cum = lax.associative_scan(_lru_combine, (a, bx), reverse=reverse, axis=1)
    h = a_cum * h0[:, None].astype(jnp.float32) + b_cum
    return h.astype(x.dtype)


def chunk_gmlp(u, v, g_v, w_s, b_s):
    B, T, _ = v.shape
    n = T // CHUNK
    vn = rms_norm(v, g_v).reshape(B, n, CHUNK, GMLP_GROUPS, GMLP_GROUP_DIM)
    mixed = jnp.einsum('gpq,bnqgc->bnpgc', w_s, vn) + b_s.T[None, None, :, :, None]
    return u * mixed.reshape(B, T, D_GMLP)


def mixer(h, h0_f, h0_b, p):
    z = h @ p['w_in']
    xr, gr, u, v, ga, gb = jnp.split(z, IN_SPLITS, axis=-1)
    xr = centred_dwconv(xr, p['conv_w'], p['conv_b'])
    hf = rg_lru(xr, h0_f, p['w_r'][0], p['b_r'][0], p['w_i'][0], p['b_i'][0], p['lam'][0], False)
    hb = rg_lru(xr, h0_b, p['w_r'][1], p['b_r'][1], p['w_i'][1], p['b_i'][1], p['lam'][1], True)
    y_rnn = jax.nn.gelu(gr) * (hf + hb)
    y_g = chunk_gmlp(jax.nn.gelu(u), jax.nn.gelu(v), p['gmlp_norm'], p['w_s'], p['b_s'])
    merged = jax.nn.sigmoid(ga) * (y_rnn @ p['w_br']) + jax.nn.sigmoid(gb) * (y_g @ p['w_bg'])
    return merged @ p['w_out'], hf[:, -1], hb[:, 0]


def trunk_layer(x, cond, h0_f, h0_b, p):
    m = (jax.nn.silu(cond) @ p['w_mod'] + p['b_mod']).reshape(cond.shape[0], N_MOD, 1, D_MODEL)
    sh1, sc1, g1 = m[:, 0], m[:, 1], m[:, 2]
    sh2, sc2, g2 = m[:, 3], m[:, 4], m[:, 5]
    sh3, sc3, g3 = m[:, 6], m[:, 7], m[:, 8]
    x = x + 0.5 * g1 * swiglu(modulate(x, p['norm1'], sh1, sc1), p['ff1_gate'], p['ff1_up'], p['ff1_down'])
    y, hf, hb = mixer(modulate(x, p['norm2'], sh2, sc2), h0_f, h0_b, p)
    x = x + g2 * y
    x = x + 0.5 * g3 * swiglu(modulate(x, p['norm3'], sh3, sc3), p['ff2_gate'], p['ff2_up'], p['ff2_down'])
    return x, hf, hb


def setup_inputs(seed: int = 0) -> dict:
    key = jax.random.key(seed)
    ks = jax.random.split(key, 40)
    nrm = lambda k, shape, s: jax.random.normal(k, shape, jnp.float32) * s
    gain = lambda k, shape: 1.0 + 0.02 * jax.random.normal(k, shape, jnp.float32)
    L = DEPTH
    a0 = jax.random.uniform(ks[18], (L, 2, D_RNN), jnp.float32, 0.9, 0.999)
    return {
        "x_prompt": nrm(ks[0], (BATCH, SEQ, D_MODEL), 1.0),
        "x_sample": nrm(ks[1], (DEC_BATCH, DEC_SEQ, D_MODEL), 1.0),
        "state_rnn_fwd": nrm(ks[2], (DEC_BATCH, DEPTH, D_RNN), 1.0),
        "state_rnn_bwd": nrm(ks[3], (DEC_BATCH, DEPTH, D_RNN), 1.0),
        "c": nrm(ks[4], (DEC_BATCH, D_MODEL), 1.0),
        "c_ctx": nrm(ks[5], (D_MODEL,), 1.0),
        "w_mod": nrm(ks[6], (L, D_MODEL, N_MOD * D_MODEL), D_MODEL ** -0.5),
        "b_mod": nrm(ks[7], (L, N_MOD * D_MODEL), 0.02),
        "norm1": gain(ks[8], (L, D_MODEL)),
        "norm2": gain(ks[9], (L, D_MODEL)),
        "norm3": gain(ks[10], (L, D_MODEL)),
        "ff1_gate": nrm(ks[11], (L, D_MODEL, D_FF), D_MODEL ** -0.5),
        "ff1_up": nrm(ks[12], (L, D_MODEL, D_FF), D_MODEL ** -0.5),
        "ff1_down": nrm(ks[13], (L, D_FF, D_MODEL), D_FF ** -0.5),
        "w_in": nrm(ks[14], (L, D_MODEL, IN_COLS), D_MODEL ** -0.5),
        "conv_w": nrm(ks[15], (L, CONV_W, D_RNN), CONV_W ** -0.5),
        "conv_b": nrm(ks[16], (L, D_RNN), 0.02),
        "w_r": nrm(ks[17], (L, 2, RNN_HEADS, RNN_HEAD_DIM, RNN_HEAD_DIM), RNN_HEAD_DIM ** -0.5),
        "b_r": nrm(ks[19], (L, 2, D_RNN), 0.02),
        "w_i": nrm(ks[20], (L, 2, RNN_HEADS, RNN_HEAD_DIM, RNN_HEAD_DIM), RNN_HEAD_DIM ** -0.5),
        "b_i": nrm(ks[21], (L, 2, D_RNN), 0.02),
        "lam": jnp.log(a0 / (1.0 - a0)),
        "gmlp_norm": gain(ks[22], (L, D_GMLP)),
        "w_s": nrm(ks[23], (L, GMLP_GROUPS, CHUNK, CHUNK), CHUNK ** -0.5),
        "b_s": gain(ks[24], (L, GMLP_GROUPS, CHUNK)),
        "w_br": nrm(ks[25], (L, D_RNN, D_MODEL), D_RNN ** -0.5),
        "w_bg": nrm(ks[26], (L, D_GMLP, D_MODEL), D_GMLP ** -0.5),
        "w_out": nrm(ks[27], (L, D_MODEL, D_MODEL), D_MODEL ** -0.5),
        "ff2_gate": nrm(ks[28], (L, D_MODEL, D_FF), D_MODEL ** -0.5),
        "ff2_up": nrm(ks[29], (L, D_MODEL, D_FF), D_MODEL ** -0.5),
        "ff2_down": nrm(ks[30], (L, D_FF, D_MODEL), D_FF ** -0.5),
        "norm_f": gain(ks[31], (D_MODEL,)),
    }


def reference(x_prompt, x_sample, state_rnn_fwd, state_rnn_bwd, c, c_ctx,
              w_mod, b_mod, norm1, norm2, norm3, ff1_gate, ff1_up, ff1_down,
              w_in, conv_w, conv_b, w_r, b_r, w_i, b_i, lam, gmlp_norm, w_s, b_s,
              w_br, w_bg, w_out, ff2_gate, ff2_up, ff2_down, norm_f):
    def layer_params(l):
        return dict(w_mod=w_mod[l], b_mod=b_mod[l], norm1=norm1[l], norm2=norm2[l], norm3=norm3[l],
                    ff1_gate=ff1_gate[l], ff1_up=ff1_up[l], ff1_down=ff1_down[l],
                    w_in=w_in[l], conv_w=conv_w[l], conv_b=conv_b[l],
                    w_r=w_r[l], b_r=b_r[l], w_i=w_i[l], b_i=b_i[l], lam=lam[l],
                    gmlp_norm=gmlp_norm[l], w_s=w_s[l], b_s=b_s[l],
                    w_br=w_br[l], w_bg=w_bg[l], w_out=w_out[l],
                    ff2_gate=ff2_gate[l], ff2_up=ff2_up[l], ff2_down=ff2_down[l])

    xc = x_prompt
    zeros = jnp.zeros((x_prompt.shape[0], D_RNN), x_prompt.dtype)
    st_f, st_b = [], []
    for l in range(DEPTH):
        xc, hf, hb = trunk_layer(xc, c_ctx[None], zeros, zeros, layer_params(l))
        st_f.append(hf)
        st_b.append(hb)
    y_prompt = rms_norm(xc, norm_f)
    new_state_rnn_fwd = jnp.stack(st_f, axis=1)
    new_state_rnn_bwd = jnp.stack(st_b, axis=1)

    xs = x_sample + grid_pos_embed(x_sample.shape[1], x_sample.dtype)[None]
    for l in range(DEPTH):
        xs, _, _ = trunk_layer(xs, c, state_rnn_fwd[:, l], state_rnn_bwd[:, l], layer_params(l))
    y_sample = rms_norm(xs, norm_f)

    return (y_prompt, y_sample, new_state_rnn_fwd, new_state_rnn_bwd)
```

```python
import functools

import jax
import jax.numpy as jnp
from jax import lax
from jax.experimental import pallas as pl
from jax.experimental.pallas import tpu as pltpu

D = 1024
D_FF = 2816
N_MOD = 9
EPS = 1e-6
CHUNK = 128
GROUPS = 8
HEADS = 16
HEAD_DIM = 64
CONV_W = 4
CONV_LEFT = 2
LRU_C = 8.0
GRID_W = 64
Z_REST = 5 * D

LANES = 128
SUBLANES = 8
N_LANE_CHUNKS = D // LANES
HEADS_PER_BLOCK = 4
GATE_BLOCK = HEADS_PER_BLOCK * HEAD_DIM
N_GATE_BLOCKS = D // GATE_BLOCK

ROWS_TOKEN_TILE = 256
SCAN_BATCH = SUBLANES
SCAN_T = 128
VMEM_LIMIT = 56 * 1024 * 1024

F32 = jnp.float32
BF16 = jnp.bfloat16


def _rms(x, g):
    ms = jnp.mean(x * x, axis=-1, keepdims=True)
    return (x * lax.rsqrt(ms + EPS)) * g


def _modulate(x, g, shift, scale):
    return _rms(x, g) * (1.0 + scale) + shift


def _dot(a, b):
    return jnp.dot(a, b, preferred_element_type=F32)


def _whole(memory_space=pltpu.VMEM):
    return pl.BlockSpec(memory_space=memory_space)


def _mod_kernel(cond_ref, w_ref, b_ref, o_ref):
    c = cond_ref[...]
    s = (c * jax.nn.sigmoid(c)).astype(BF16)
    o_ref[...] = _dot(s, w_ref[...].astype(BF16)) + b_ref[...]


def _modulation(cond, w_mod, b_mod):
    rows = cond.shape[0]
    n = w_mod.shape[1]
    bn = D
    return pl.pallas_call(
        _mod_kernel,
        grid=(n // bn,),
        in_specs=[
            pl.BlockSpec((rows, D), lambda j: (0, 0)),
            pl.BlockSpec((D, bn), lambda j: (0, j)),
            pl.BlockSpec((1, bn), lambda j: (0, j)),
        ],
        out_specs=pl.BlockSpec((rows, bn), lambda j: (0, j)),
        out_shape=jax.ShapeDtypeStruct((rows, n), F32),
        compiler_params=pltpu.CompilerParams(dimension_semantics=("arbitrary",)),
        name="modulation",
    )(cond, w_mod, b_mod.reshape(1, n))


def _stage1_kernel(has_pe, *refs):
    if has_pe:
        x_ref, pe_ref, refs = refs[0], refs[1], refs[2:]
    else:
        x_ref, pe_ref, refs = refs[0], None, refs[1:]
    (m_ref, n1_ref, n2_ref, wg_ref, wu_ref, wd_ref, win_ref,
     x1_ref, zx_ref, zr_ref) = refs
    bb, tt, _ = x_ref.shape
    x = x_ref[...]
    if has_pe:
        x = x + pe_ref[...][None]
    x = x.reshape(bb * tt, D)
    m = m_ref[0]
    h = _modulate(x, n1_ref[...], m[0:1], m[1:2]).astype(BF16)
    g = _dot(h, wg_ref[...])
    u = _dot(h, wu_ref[...])
    act = ((g * jax.nn.sigmoid(g)) * u).astype(BF16)
    x1 = x + (0.5 * m[2:3]) * _dot(act, wd_ref[...])
    x1_ref[...] = x1.reshape(bb, tt, D)
    h2 = _modulate(x1, n2_ref[...], m[3:4], m[4:5]).astype(BF16)
    zx_ref[...] = _dot(h2, win_ref[:, 0:D]).reshape(bb, tt, D)
    for c in range(Z_REST // D):
        lo = D + c * D
        zr_ref[:, :, c * D:(c + 1) * D] = (
            _dot(h2, win_ref[:, lo:lo + D]).astype(BF16).reshape(bb, tt, D))


def _stage1(x, pe, m, n1, n2, wg, wu, wd, win, bb, tt):
    B, T, _ = x.shape
    has_pe = pe is not None
    per_batch_m = m.shape[0] > 1
    grid = (T // tt, B // bb)
    m_map = (lambda t, b: (b, 0, 0)) if per_batch_m else (lambda t, b: (0, 0, 0))
    in_specs = [pl.BlockSpec((bb, tt, D), lambda t, b: (b, t, 0))]
    args = [x]
    if has_pe:
        in_specs.append(pl.BlockSpec((tt, D), lambda t, b: (t, 0)))
        args.append(pe)
    in_specs += [pl.BlockSpec((1, N_MOD, D), m_map)] + [_whole()] * 6
    args += [m, n1, n2, wg, wu, wd, win]
    blk = lambda w: pl.BlockSpec((bb, tt, w), lambda t, b: (b, t, 0))
    return pl.pallas_call(
        functools.partial(_stage1_kernel, has_pe),
        grid=grid,
        in_specs=in_specs,
        out_specs=[blk(D), blk(D), blk(Z_REST)],
        out_shape=[jax.ShapeDtypeStruct((B, T, D), F32),
                   jax.ShapeDtypeStruct((B, T, D), F32),
                   jax.ShapeDtypeStruct((B, T, Z_REST), BF16)],
        compiler_params=pltpu.CompilerParams(
            dimension_semantics=("arbitrary", "arbitrary"),
            vmem_limit_bytes=VMEM_LIMIT),
        name="stage1_ffn_inproj",
    )(*args)


def _log_sigmoid(x):
    return jnp.minimum(x, 0.0) - jnp.log1p(jnp.exp(-jnp.abs(x)))


def _stage2_kernel(reverse, zc_ref, zl_ref, zrt_ref, h0_ref, cw_ref, cb_ref,
                   wg_ref, bg_ref, lam_ref, h_ref,
                   ext_scr, a_scr, b_scr, o_scr, carry_scr):
    nb, tt, _ = zc_ref.shape
    j = pl.program_id(1)
    nt = pl.num_programs(1)
    ti = (nt - 1 - j) if reverse else j
    halo = SUBLANES

    ext_scr[:, halo:halo + tt, :] = zc_ref[...]
    ext_scr[:, 0:halo, :] = jnp.where(ti > 0, zl_ref[...], 0.0)
    ext_scr[:, halo + tt:halo + tt + halo, :] = jnp.where(ti < nt - 1, zrt_ref[...], 0.0)

    @pl.when(j == 0)
    def _():
        for c in range(N_LANE_CHUNKS):
            carry_scr[c] = h0_ref[:, c * LANES:(c + 1) * LANES]

    lam_ls = _log_sigmoid(lam_ref[...])
    for blk in range(N_GATE_BLOCKS):
        lo = blk * GATE_BLOCK
        xr = cb_ref[:, lo:lo + GATE_BLOCK][None]
        for k in range(CONV_W):
            off = halo - CONV_LEFT + k
            xr = xr + ext_scr[:, off:off + tt, lo:lo + GATE_BLOCK] * cw_ref[k:k + 1, lo:lo + GATE_BLOCK][None]
        xr = xr.reshape(nb * tt, GATE_BLOCK)
        pre = _dot(xr.astype(BF16), wg_ref[blk]) + bg_ref[blk]
        r = jax.nn.sigmoid(pre[:, 0:GATE_BLOCK])
        i = jax.nn.sigmoid(pre[:, GATE_BLOCK:2 * GATE_BLOCK])
        log_a = (LRU_C * r) * lam_ls[:, lo:lo + GATE_BLOCK]
        a = jnp.exp(log_a)
        mult = jnp.sqrt(jnp.maximum(-jnp.tanh(log_a) * (a * a + 1.0), 0.0))
        bx = mult * (i * xr)
        for half in range(GATE_BLOCK // LANES):
            c = blk * (GATE_BLOCK // LANES) + half
            for s in range(nb):
                rows = slice(s * tt, (s + 1) * tt)
                cols = slice(half * LANES, (half + 1) * LANES)
                a_scr[c, pl.ds(s, tt, stride=nb), :] = a[rows, cols]
                b_scr[c, pl.ds(s, tt, stride=nb), :] = bx[rows, cols]

    def step(s, hs):
        t = (tt - 1 - s) if reverse else s
        r0 = pl.multiple_of(t * nb, SUBLANES)
        out = []
        for c in range(N_LANE_CHUNKS):
            h = a_scr[c, pl.ds(r0, nb), :] * hs[c] + b_scr[c, pl.ds(r0, nb), :]
            o_scr[c, pl.ds(r0, nb), :] = h
            out.append(h)
        return tuple(out)

    hs = lax.fori_loop(0, tt, step,
                       tuple(carry_scr[c] for c in range(N_LANE_CHUNKS)), unroll=8)
    for c in range(N_LANE_CHUNKS):
        carry_scr[c] = hs[c]
        for s in range(nb):
            h_ref[s, :, c * LANES:(c + 1) * LANES] = o_scr[c, pl.ds(s, tt, stride=nb), :]


def _stage2(zx, h0, conv_w, conv_b, wgate, bgate, lam, reverse, tt):
    B, T, _ = zx.shape
    nb = SCAN_BATCH
    nt = T // tt
    tpb = tt // SUBLANES
    nhb = T // SUBLANES

    def ti(j):
        return (nt - 1 - j) if reverse else j

    return pl.pallas_call(
        functools.partial(_stage2_kernel, reverse),
        grid=(B // nb, nt),
        in_specs=[
            pl.BlockSpec((nb, tt, D), lambda g, j: (g, ti(j), 0)),
            pl.BlockSpec((nb, SUBLANES, D),
                         lambda g, j: (g, jnp.maximum(ti(j) * tpb - 1, 0), 0)),
            pl.BlockSpec((nb, SUBLANES, D),
                         lambda g, j: (g, jnp.minimum((ti(j) + 1) * tpb, nhb - 1), 0)),
            pl.BlockSpec((nb, D), lambda g, j: (g, 0)),
            _whole(), _whole(), _whole(), _whole(), _whole(),
        ],
        out_specs=pl.BlockSpec((nb, tt, D), lambda g, j: (g, ti(j), 0)),
        out_shape=jax.ShapeDtypeStruct((B, T, D), F32),
        scratch_shapes=[
            pltpu.VMEM((nb, tt + 2 * SUBLANES, D), F32),
            pltpu.VMEM((N_LANE_CHUNKS, tt * nb, LANES), F32),
            pltpu.VMEM((N_LANE_CHUNKS, tt * nb, LANES), F32),
            pltpu.VMEM((N_LANE_CHUNKS, tt * nb, LANES), F32),
            pltpu.VMEM((N_LANE_CHUNKS, nb, LANES), F32),
        ],
        compiler_params=pltpu.CompilerParams(
            dimension_semantics=("arbitrary", "arbitrary"),
            vmem_limit_bytes=VMEM_LIMIT),
        name="stage2_scan_bwd" if reverse else "stage2_scan_fwd",
    )(zx, zx, zx, h0, conv_w, conv_b, wgate, bgate, lam)


def _stage3_kernel(x1_ref, hf_ref, hb_ref, zr_ref, m_ref, gv_ref, ws_ref, bs_ref,
                   wbr_ref, wbg_ref, wo_ref, n3_ref, wg_ref, wu_ref, wd_ref, nf_ref,
                   y_ref, mixed_scr):
    bb, tt, _ = x1_ref.shape
    rows = bb * tt
    m = m_ref[0]

    def zcol(c):
        return zr_ref[:, :, c * D:(c + 1) * D].reshape(rows, D).astype(F32)

    y_rnn = jax.nn.gelu(zcol(0)) * (hf_ref[...] + hb_ref[...]).reshape(rows, D)
    rnn_proj = _dot(y_rnn.astype(BF16), wbr_ref[...])

    u = jax.nn.gelu(zcol(1))
    vn = _rms(jax.nn.gelu(zcol(2)), gv_ref[...]).astype(BF16)
    for n in range(rows // CHUNK):
        rs = slice(n * CHUNK, (n + 1) * CHUNK)
        for g in range(GROUPS):
            cs = slice(g * LANES, (g + 1) * LANES)
            mixed_scr[rs, cs] = _dot(ws_ref[g], vn[rs, cs]) + bs_ref[:, cs]
    y_g = u * mixed_scr[...]
    g_proj = _dot(y_g.astype(BF16), wbg_ref[...])

    merged = jax.nn.sigmoid(zcol(3)) * rnn_proj + jax.nn.sigmoid(zcol(4)) * g_proj
    x2 = x1_ref[...].reshape(rows, D) + m[5:6] * _dot(merged.astype(BF16), wo_ref[...])

    h = _modulate(x2, n3_ref[...], m[6:7], m[7:8]).astype(BF16)
    g = _dot(h, wg_ref[...])
    uu = _dot(h, wu_ref[...])
    act = ((g * jax.nn.sigmoid(g)) * uu).astype(BF16)
    x3 = x2 + (0.5 * m[8:9]) * _dot(act, wd_ref[...])
    y_ref[...] = _rms(x3, nf_ref[...]).reshape(bb, tt, D)


def _stage3(x1, hf, hb, zr, m, gv, ws, bs, wbr, wbg, wo, n3, wg, wu, wd, nf, bb, tt):
    B, T, _ = x1.shape
    per_batch_m = m.shape[0] > 1
    m_map = (lambda t, b: (b, 0, 0)) if per_batch_m else (lambda t, b: (0, 0, 0))
    blk = lambda w: pl.BlockSpec((bb, tt, w), lambda t, b: (b, t, 0))
    return pl.pallas_call(
        _stage3_kernel,
        grid=(T // tt, B // bb),
        in_specs=[blk(D), blk(D), blk(D), blk(Z_REST),
                  pl.BlockSpec((1, N_MOD, D), m_map)] + [_whole()] * 11,
        out_specs=blk(D),
        out_shape=jax.ShapeDtypeStruct((B, T, D), F32),
        scratch_shapes=[pltpu.VMEM((bb * tt, D), F32)],
        compiler_params=pltpu.CompilerParams(
            dimension_semantics=("arbitrary", "arbitrary"),
            vmem_limit_bytes=VMEM_LIMIT),
        name="stage3_mix_ffn",
    )(x1, hf, hb, zr, m, gv, ws, bs, wbr, wbg, wo, n3, wg, wu, wd, nf)


def _grid_pos_embed(n_tokens):
    rows = n_tokens // GRID_W
    t = jnp.arange(rows * GRID_W)
    r = (t // GRID_W).astype(F32)
    col = (t % GRID_W).astype(F32)
    q = D // 4
    freqs = 1.0 / (10000.0 ** (jnp.arange(q, dtype=F32) / q))
    ang_r = r[:, None] * freqs
    ang_c = col[:, None] * freqs
    return jnp.concatenate([jnp.sin(ang_r), jnp.cos(ang_r), jnp.sin(ang_c), jnp.cos(ang_c)], axis=-1)


def _gate_weights(w_r, w_i, b_r, b_i):
    def bd(w):
        w4 = w.reshape(N_GATE_BLOCKS, HEADS_PER_BLOCK, HEAD_DIM, HEAD_DIM)
        eye = jnp.eye(HEADS_PER_BLOCK, dtype=w.dtype)
        return jnp.einsum('ghij,hk->ghikj', w4, eye).reshape(N_GATE_BLOCKS, GATE_BLOCK, GATE_BLOCK)
    wg = jnp.concatenate([bd(w_r), bd(w_i)], axis=-1).astype(BF16)
    bg = jnp.concatenate([b_r.reshape(N_GATE_BLOCKS, 1, GATE_BLOCK),
                          b_i.reshape(N_GATE_BLOCKS, 1, GATE_BLOCK)], axis=-1)
    return wg, bg


def _token_tile(T):
    tt = min(T, ROWS_TOKEN_TILE)
    return ROWS_TOKEN_TILE // tt, tt


def kernel(x_prompt, x_sample, state_rnn_fwd, state_rnn_bwd, c, c_ctx, w_mod, b_mod, norm1, norm2, norm3, ff1_gate, ff1_up, ff1_down, w_in, conv_w, conv_b, w_r, b_r, w_i, b_i, lam, gmlp_norm, w_s, b_s, w_br, w_bg, w_out, ff2_gate, ff2_up, ff2_down, norm_f):
    l = 0
    nbatch_lat = c.shape[0]
    cond_rows = 2 * SUBLANES
    cond = jnp.zeros((cond_rows, D), F32).at[:nbatch_lat].set(c).at[nbatch_lat].set(c_ctx)
    m_all = _modulation(cond, w_mod[l], b_mod[l])
    m_lat = m_all[:nbatch_lat].reshape(nbatch_lat, N_MOD, D)
    m_ctx = m_all[nbatch_lat:nbatch_lat + 1].reshape(1, N_MOD, D)

    row = lambda v: v.reshape(1, D)
    bf = lambda w: w.astype(BF16)
    n1, n2, n3, nf, gv = row(norm1[l]), row(norm2[l]), row(norm3[l]), row(norm_f), row(gmlp_norm[l])
    wg1, wu1, wd1 = bf(ff1_gate[l]), bf(ff1_up[l]), bf(ff1_down[l])
    wg2, wu2, wd2 = bf(ff2_gate[l]), bf(ff2_up[l]), bf(ff2_down[l])
    win, wbr, wbg, wo = bf(w_in[l]), bf(w_br[l]), bf(w_bg[l]), bf(w_out[l])
    ws = bf(w_s[l])
    bs = jnp.repeat(b_s[l].T, D // GROUPS, axis=1)
    cw, cb, lam_l = conv_w[l], row(conv_b[l]), lam[l]
    gates = [_gate_weights(w_r[l, d], w_i[l, d], b_r[l, d], b_i[l, d]) for d in range(2)]

    def trunk(x, pe, m, h0_f, h0_b):
        B, T, _ = x.shape
        bb, tt = _token_tile(T)
        x1, zx, zr = _stage1(x, pe, m, n1, n2, wg1, wu1, wd1, win, bb, tt)
        st = min(T, SCAN_T)
        hf = _stage2(zx, h0_f, cw, cb, gates[0][0], gates[0][1], row(lam_l[0]), False, st)
        hb = _stage2(zx, h0_b, cw, cb, gates[1][0], gates[1][1], row(lam_l[1]), True, st)
        y = _stage3(x1, hf, hb, zr, m, gv, ws, bs, wbr, wbg, wo, n3, wg2, wu2, wd2, nf, bb, tt)
        return y, hf, hb

    zeros = jnp.zeros((x_prompt.shape[0], D), F32)
    y_prompt, hf_c, hb_c = trunk(x_prompt, None, m_ctx, zeros, zeros)
    new_f = hf_c[:, -1:, :]
    new_b = hb_c[:, 0:1, :]

    pe = _grid_pos_embed(x_sample.shape[1])
    y_sample, _, _ = trunk(x_sample, pe, m_lat, state_rnn_fwd[:, l], state_rnn_bwd[:, l])
    return (y_prompt, y_sample, new_f, new_b)
```

```python
import functools

import jax
import jax.numpy as jnp
from jax import lax
from jax.experimental import pallas as pl
from jax.experimental.pallas import tpu as pltpu

D = 1024
D_FF = 2816
N_MOD = 9
EPS = 1e-6
CHUNK = 128
GROUPS = 8
HEADS = 16
HEAD_DIM = 64
CONV_W = 4
CONV_LEFT = 2
LRU_C = 8.0
GRID_W = 64
Z_REST = 5 * D

LANES = 128
SUBLANES = 8
N_LANE_CHUNKS = D // LANES
HEADS_PER_BLOCK = 4
GATE_BLOCK = HEADS_PER_BLOCK * HEAD_DIM
N_GATE_BLOCKS = D // GATE_BLOCK

ROWS_TOKEN_TILE = 256
SCAN_BATCH = SUBLANES
SCAN_T = 128
VMEM_LIMIT = 56 * 1024 * 1024

F32 = jnp.float32
BF16 = jnp.bfloat16


def _rms(x, g):
    ms = jnp.mean(x * x, axis=-1, keepdims=True)
    return (x * lax.rsqrt(ms + EPS)) * g


def _modulate(x, g, shift, scale):
    return _rms(x, g) * (1.0 + scale) + shift


def _sigmoid(x):
    return 0.5 * jnp.tanh(0.5 * x) + 0.5


def _dot(a, b):
    return jnp.dot(a, b, preferred_element_type=F32)


def _whole(memory_space=pltpu.VMEM):
    return pl.BlockSpec(memory_space=memory_space)


def _mod_kernel(cond_ref, w_ref, b_ref, o_ref):
    c = cond_ref[...]
    s = (c * _sigmoid(c)).astype(BF16)
    o_ref[...] = _dot(s, w_ref[...].astype(BF16)) + b_ref[...]


def _modulation(cond, w_mod, b_mod):
    rows = cond.shape[0]
    n = w_mod.shape[1]
    bn = D
    return pl.pallas_call(
        _mod_kernel,
        grid=(n // bn,),
        in_specs=[
            pl.BlockSpec((rows, D), lambda j: (0, 0)),
            pl.BlockSpec((D, bn), lambda j: (0, j)),
            pl.BlockSpec((1, bn), lambda j: (0, j)),
        ],
        out_specs=pl.BlockSpec((rows, bn), lambda j: (0, j)),
        out_shape=jax.ShapeDtypeStruct((rows, n), F32),
        compiler_params=pltpu.CompilerParams(dimension_semantics=("arbitrary",)),
        name="modulation",
    )(cond, w_mod, b_mod.reshape(1, n))


def _stage1_kernel(has_pe, *refs):
    if has_pe:
        x_ref, pe_ref, refs = refs[0], refs[1], refs[2:]
    else:
        x_ref, pe_ref, refs = refs[0], None, refs[1:]
    (m_ref, n1_ref, n2_ref, wg_ref, wu_ref, wd_ref, win_ref,
     x1_ref, zx_ref, zr_ref) = refs
    bb, tt, _ = x_ref.shape
    x = x_ref[...]
    if has_pe:
        x = x + pe_ref[...][None]
    x = x.reshape(bb * tt, D)
    m = m_ref[0]
    h = _modulate(x, n1_ref[...], m[0:1], m[1:2]).astype(BF16)
    g = _dot(h, wg_ref[...])
    u = _dot(h, wu_ref[...])
    act = ((g * _sigmoid(g)) * u).astype(BF16)
    x1 = x + (0.5 * m[2:3]) * _dot(act, wd_ref[...])
    x1_ref[...] = x1.reshape(bb, tt, D)
    h2 = _modulate(x1, n2_ref[...], m[3:4], m[4:5]).astype(BF16)
    zx_ref[...] = _dot(h2, win_ref[:, 0:D]).reshape(bb, tt, D)
    for c in range(Z_REST // D):
        lo = D + c * D
        zr_ref[:, :, c * D:(c + 1) * D] = (
            _dot(h2, win_ref[:, lo:lo + D]).astype(BF16).reshape(bb, tt, D))


def _stage1(x, pe, m, n1, n2, wg, wu, wd, win, bb, tt):
    B, T, _ = x.shape
    has_pe = pe is not None
    per_batch_m = m.shape[0] > 1
    grid = (T // tt, B // bb)
    m_map = (lambda t, b: (b, 0, 0)) if per_batch_m else (lambda t, b: (0, 0, 0))
    in_specs = [pl.BlockSpec((bb, tt, D), lambda t, b: (b, t, 0))]
    args = [x]
    if has_pe:
        in_specs.append(pl.BlockSpec((tt, D), lambda t, b: (t, 0)))
        args.append(pe)
    in_specs += [pl.BlockSpec((1, N_MOD, D), m_map)] + [_whole()] * 6
    args += [m, n1, n2, wg, wu, wd, win]
    blk = lambda w: pl.BlockSpec((bb, tt, w), lambda t, b: (b, t, 0))
    return pl.pallas_call(
        functools.partial(_stage1_kernel, has_pe),
        grid=grid,
        in_specs=in_specs,
        out_specs=[blk(D), blk(D), blk(Z_REST)],
        out_shape=[jax.ShapeDtypeStruct((B, T, D), F32),
                   jax.ShapeDtypeStruct((B, T, D), F32),
                   jax.ShapeDtypeStruct((B, T, Z_REST), BF16)],
        compiler_params=pltpu.CompilerParams(
            dimension_semantics=("arbitrary", "arbitrary"),
            vmem_limit_bytes=VMEM_LIMIT),
        name="stage1_ffn_inproj",
    )(*args)


def _log_sigmoid(x):
    return jnp.minimum(x, 0.0) - jnp.log1p(jnp.exp(-jnp.abs(x)))


def _stage2_kernel(reverse, zc_ref, zl_ref, zrt_ref, h0_ref, cw_ref, cb_ref,
                   wg_ref, bg_ref, lam_ref, h_ref,
                   ext_scr, a_scr, b_scr, o_scr, carry_scr):
    nb, tt, _ = zc_ref.shape
    j = pl.program_id(1)
    nt = pl.num_programs(1)
    ti = (nt - 1 - j) if reverse else j
    rows = tt * nb
    lead = CONV_LEFT * nb

    @pl.when(j == 0)
    def _():
        for c in range(N_LANE_CHUNKS):
            carry_scr[c] = h0_ref[:, c * LANES:(c + 1) * LANES]

    for c in range(N_LANE_CHUNKS):
        cols = slice(c * LANES, (c + 1) * LANES)
        for k in range(CONV_LEFT):
            ext_scr[c, k * nb:(k + 1) * nb, :] = jnp.where(
                ti > 0, zl_ref[:, SUBLANES - CONV_LEFT + k, cols], 0.0)
        for k in range(CONV_W - 1 - CONV_LEFT):
            ext_scr[c, lead + rows + k * nb:lead + rows + (k + 1) * nb, :] = jnp.where(
                ti < nt - 1, zrt_ref[:, k, cols], 0.0)
        for s in range(nb):
            ext_scr[c, pl.ds(lead + s, tt, stride=nb), :] = zc_ref[s, :, cols]

    lam_ls = _log_sigmoid(lam_ref[...])
    for blk in range(N_GATE_BLOCKS):
        lo = blk * GATE_BLOCK
        parts = []
        for half in range(GATE_BLOCK // LANES):
            c = blk * (GATE_BLOCK // LANES) + half
            cols = slice(c * LANES, (c + 1) * LANES)
            acc = ext_scr[c, 0:rows, :] * cw_ref[0:1, cols] + cb_ref[:, cols]
            for k in range(1, CONV_W):
                acc = acc + ext_scr[c, k * nb:k * nb + rows, :] * cw_ref[k:k + 1, cols]
            parts.append(acc)
        xr = jnp.concatenate(parts, axis=1)
        pre = _dot(xr.astype(BF16), wg_ref[blk]) + bg_ref[blk]
        r = _sigmoid(pre[:, 0:GATE_BLOCK])
        i = _sigmoid(pre[:, GATE_BLOCK:2 * GATE_BLOCK])
        log_a = (LRU_C * r) * lam_ls[:, lo:lo + GATE_BLOCK]
        a = jnp.exp(log_a)
        mult = jnp.sqrt(jnp.maximum(-jnp.tanh(log_a) * (a * a + 1.0), 0.0))
        bx = mult * (i * xr)
        for half in range(GATE_BLOCK // LANES):
            c = blk * (GATE_BLOCK // LANES) + half
            a_scr[c] = a[:, half * LANES:(half + 1) * LANES]
            b_scr[c] = bx[:, half * LANES:(half + 1) * LANES]

    def step(s, hs):
        t = (tt - 1 - s) if reverse else s
        r0 = pl.multiple_of(t * nb, SUBLANES)
        out = []
        for c in range(N_LANE_CHUNKS):
            h = a_scr[c, pl.ds(r0, nb), :] * hs[c] + b_scr[c, pl.ds(r0, nb), :]
            o_scr[c, pl.ds(r0, nb), :] = h
            out.append(h)
        return tuple(out)

    hs = lax.fori_loop(0, tt, step,
                       tuple(carry_scr[c] for c in range(N_LANE_CHUNKS)), unroll=8)
    for c in range(N_LANE_CHUNKS):
        carry_scr[c] = hs[c]
        for s in range(nb):
            h_ref[s, :, c * LANES:(c + 1) * LANES] = o_scr[c, pl.ds(s, tt, stride=nb), :]


def _stage2(zx, h0, conv_w, conv_b, wgate, bgate, lam, reverse, tt):
    B, T, _ = zx.shape
    nb = SCAN_BATCH
    nt = T // tt
    tpb = tt // SUBLANES
    nhb = T // SUBLANES

    def ti(j):
        return (nt - 1 - j) if reverse else j

    return pl.pallas_call(
        functools.partial(_stage2_kernel, reverse),
        grid=(B // nb, nt),
        in_specs=[
            pl.BlockSpec((nb, tt, D), lambda g, j: (g, ti(j), 0)),
            pl.BlockSpec((nb, SUBLANES, D),
                         lambda g, j: (g, jnp.maximum(ti(j) * tpb - 1, 0), 0)),
            pl.BlockSpec((nb, SUBLANES, D),
                         lambda g, j: (g, jnp.minimum((ti(j) + 1) * tpb, nhb - 1), 0)),
            pl.BlockSpec((nb, D), lambda g, j: (g, 0)),
            _whole(), _whole(), _whole(), _whole(), _whole(),
        ],
        out_specs=pl.BlockSpec((nb, tt, D), lambda g, j: (g, ti(j), 0)),
        out_shape=jax.ShapeDtypeStruct((B, T, D), F32),
        scratch_shapes=[
            pltpu.VMEM((N_LANE_CHUNKS, (tt + CONV_W - 1) * nb, LANES), F32),
            pltpu.VMEM((N_LANE_CHUNKS, tt * nb, LANES), F32),
            pltpu.VMEM((N_LANE_CHUNKS, tt * nb, LANES), F32),
            pltpu.VMEM((N_LANE_CHUNKS, tt * nb, LANES), F32),
            pltpu.VMEM((N_LANE_CHUNKS, nb, LANES), F32),
        ],
        compiler_params=pltpu.CompilerParams(
            dimension_semantics=("arbitrary", "arbitrary"),
            vmem_limit_bytes=VMEM_LIMIT),
        name="stage2_scan_bwd" if reverse else "stage2_scan_fwd",
    )(zx, zx, zx, h0, conv_w, conv_b, wgate, bgate, lam)


def _stage3_kernel(x1_ref, hf_ref, hb_ref, zr_ref, m_ref, gv_ref, ws_ref, bs_ref,
                   wbr_ref, wbg_ref, wo_ref, n3_ref, wg_ref, wu_ref, wd_ref, nf_ref,
                   y_ref, mixed_scr):
    bb, tt, _ = x1_ref.shape
    rows = bb * tt
    m = m_ref[0]

    def zcol(c):
        return zr_ref[:, :, c * D:(c + 1) * D].reshape(rows, D).astype(F32)

    y_rnn = jax.nn.gelu(zcol(0)) * (hf_ref[...] + hb_ref[...]).reshape(rows, D)
    rnn_proj = _dot(y_rnn.astype(BF16), wbr_ref[...])

    u = jax.nn.gelu(zcol(1))
    vn = _rms(jax.nn.gelu(zcol(2)), gv_ref[...]).astype(BF16)
    for n in range(rows // CHUNK):
        rs = slice(n * CHUNK, (n + 1) * CHUNK)
        for g in range(GROUPS):
            cs = slice(g * LANES, (g + 1) * LANES)
            mixed_scr[rs, cs] = _dot(ws_ref[g], vn[rs, cs]) + bs_ref[:, cs]
    y_g = u * mixed_scr[...]
    g_proj = _dot(y_g.astype(BF16), wbg_ref[...])

    merged = _sigmoid(zcol(3)) * rnn_proj + _sigmoid(zcol(4)) * g_proj
    x2 = x1_ref[...].reshape(rows, D) + m[5:6] * _dot(merged.astype(BF16), wo_ref[...])

    h = _modulate(x2, n3_ref[...], m[6:7], m[7:8]).astype(BF16)
    g = _dot(h, wg_ref[...])
    uu = _dot(h, wu_ref[...])
    act = ((g * _sigmoid(g)) * uu).astype(BF16)
    x3 = x2 + (0.5 * m[8:9]) * _dot(act, wd_ref[...])
    y_ref[...] = _rms(x3, nf_ref[...]).reshape(bb, tt, D)


def _stage3(x1, hf, hb, zr, m, gv, ws, bs, wbr, wbg, wo, n3, wg, wu, wd, nf, bb, tt):
    B, T, _ = x1.shape
    per_batch_m = m.shape[0] > 1
    m_map = (lambda t, b: (b, 0, 0)) if per_batch_m else (lambda t, b: (0, 0, 0))
    blk = lambda w: pl.BlockSpec((bb, tt, w), lambda t, b: (b, t, 0))
    return pl.pallas_call(
        _stage3_kernel,
        grid=(T // tt, B // bb),
        in_specs=[blk(D), blk(D), blk(D), blk(Z_REST),
                  pl.BlockSpec((1, N_MOD, D), m_map)] + [_whole()] * 11,
        out_specs=blk(D),
        out_shape=jax.ShapeDtypeStruct((B, T, D), F32),
        scratch_shapes=[pltpu.VMEM((bb * tt, D), F32)],
        compiler_params=pltpu.CompilerParams(
            dimension_semantics=("arbitrary", "arbitrary"),
            vmem_limit_bytes=VMEM_LIMIT),
        name="stage3_mix_ffn",
    )(x1, hf, hb, zr, m, gv, ws, bs, wbr, wbg, wo, n3, wg, wu, wd, nf)


def _grid_pos_embed(n_tokens):
    rows = n_tokens // GRID_W
    t = jnp.arange(rows * GRID_W)
    r = (t // GRID_W).astype(F32)
    col = (t % GRID_W).astype(F32)
    q = D // 4
    freqs = 1.0 / (10000.0 ** (jnp.arange(q, dtype=F32) / q))
    ang_r = r[:, None] * freqs
    ang_c = col[:, None] * freqs
    return jnp.concatenate([jnp.sin(ang_r), jnp.cos(ang_r), jnp.sin(ang_c), jnp.cos(ang_c)], axis=-1)


def _gate_weights(w_r, w_i, b_r, b_i):
    def bd(w):
        w4 = w.reshape(N_GATE_BLOCKS, HEADS_PER_BLOCK, HEAD_DIM, HEAD_DIM)
        eye = jnp.eye(HEADS_PER_BLOCK, dtype=w.dtype)
        return jnp.einsum('ghij,hk->ghikj', w4, eye).reshape(N_GATE_BLOCKS, GATE_BLOCK, GATE_BLOCK)
    wg = jnp.concatenate([bd(w_r), bd(w_i)], axis=-1).astype(BF16)
    bg = jnp.concatenate([b_r.reshape(N_GATE_BLOCKS, 1, GATE_BLOCK),
                          b_i.reshape(N_GATE_BLOCKS, 1, GATE_BLOCK)], axis=-1)
    return wg, bg


def _token_tile(T):
    tt = min(T, ROWS_TOKEN_TILE)
    return ROWS_TOKEN_TILE // tt, tt


def kernel(x_prompt, x_sample, state_rnn_fwd, state_rnn_bwd, c, c_ctx, w_mod, b_mod, norm1, norm2, norm3, ff1_gate, ff1_up, ff1_down, w_in, conv_w, conv_b, w_r, b_r, w_i, b_i, lam, gmlp_norm, w_s, b_s, w_br, w_bg, w_out, ff2_gate, ff2_up, ff2_down, norm_f):
    l = 0
    nbatch_lat = c.shape[0]
    cond_rows = 2 * SUBLANES
    cond = jnp.zeros((cond_rows, D), F32).at[:nbatch_lat].set(c).at[nbatch_lat].set(c_ctx)
    m_all = _modulation(cond, w_mod[l], b_mod[l])
    m_lat = m_all[:nbatch_lat].reshape(nbatch_lat, N_MOD, D)
    m_ctx = m_all[nbatch_lat:nbatch_lat + 1].reshape(1, N_MOD, D)

    row = lambda v: v.reshape(1, D)
    bf = lambda w: w.astype(BF16)
    n1, n2, n3, nf, gv = row(norm1[l]), row(norm2[l]), row(norm3[l]), row(norm_f), row(gmlp_norm[l])
    wg1, wu1, wd1 = bf(ff1_gate[l]), bf(ff1_up[l]), bf(ff1_down[l])
    wg2, wu2, wd2 = bf(ff2_gate[l]), bf(ff2_up[l]), bf(ff2_down[l])
    win, wbr, wbg, wo = bf(w_in[l]), bf(w_br[l]), bf(w_bg[l]), bf(w_out[l])
    ws = bf(w_s[l])
    bs = jnp.repeat(b_s[l].T, D // GROUPS, axis=1)
    cw, cb, lam_l = conv_w[l], row(conv_b[l]), lam[l]
    gates = [_gate_weights(w_r[l, d], w_i[l, d], b_r[l, d], b_i[l, d]) for d in range(2)]

    def trunk(x, pe, m, h0_f, h0_b):
        B, T, _ = x.shape
        bb, tt = _token_tile(T)
        x1, zx, zr = _stage1(x, pe, m, n1, n2, wg1, wu1, wd1, win, bb, tt)
        st = min(T, SCAN_T)
        hf = _stage2(zx, h0_f, cw, cb, gates[0][0], gates[0][1], row(lam_l[0]), False, st)
        hb = _stage2(zx, h0_b, cw, cb, gates[1][0], gates[1][1], row(lam_l[1]), True, st)
        y = _stage3(x1, hf, hb, zr, m, gv, ws, bs, wbr, wbg, wo, n3, wg2, wu2, wd2, nf, bb, tt)
        return y, hf, hb

    zeros = jnp.zeros((x_prompt.shape[0], D), F32)
    y_prompt, hf_c, hb_c = trunk(x_prompt, None, m_ctx, zeros, zeros)
    new_f = hf_c[:, -1:, :]
    new_b = hb_c[:, 0:1, :]

    pe = _grid_pos_embed(x_sample.shape[1])
    y_sample, _, _ = trunk(x_sample, pe, m_lat, state_rnn_fwd[:, l], state_rnn_bwd[:, l])
    return (y_prompt, y_sample, new_f, new_b)
```

```python
import functools

import jax
import jax.numpy as jnp
from jax import lax
from jax.experimental import pallas as pl
from jax.experimental.pallas import tpu as pltpu

D = 1024
D_FF = 2816
N_MOD = 9
EPS = 1e-6
CHUNK = 128
GROUPS = 8
HEADS = 16
HEAD_DIM = 64
CONV_W = 4
CONV_LEFT = 2
CONV_RIGHT = CONV_W - 1 - CONV_LEFT
LRU_C = 8.0
GRID_W = 64
Z_ACT = 5 * D

LANES = 128
SUBLANES = 8
N_LANE_CHUNKS = D // LANES
HEADS_PER_BLOCK = 4
GATE_BLOCK = HEADS_PER_BLOCK * HEAD_DIM
N_GATE_BLOCKS = D // GATE_BLOCK
GATE_HALVES = GATE_BLOCK // LANES
FF_CHUNKS = ((0, 768), (768, 1536), (1536, 2304), (2304, D_FF))
assert len(FF_CHUNKS) == N_GATE_BLOCKS

SCAN_BATCH = SUBLANES
SCAN_T = 32
IN_LAG = 2
IN_RING = 4
MIX_T = CHUNK
MIX_SEQS = 2
MIX_LAG = SCAN_BATCH // MIX_SEQS
assert MIX_LAG * SCAN_T == MIX_T
VMEM_LIMIT = 56 * 1024 * 1024

F32 = jnp.float32
BF16 = jnp.bfloat16


def _rms(x, g):
    ms = jnp.mean(x * x, axis=-1, keepdims=True)
    return (x * lax.rsqrt(ms + EPS)) * g


def _modulate(x, g, shift, scale):
    return _rms(x, g) * (1.0 + scale) + shift


def _sigmoid(x):
    return 0.5 * jnp.tanh(0.5 * x) + 0.5


def _log_sigmoid(x):
    return jnp.minimum(x, 0.0) - jnp.log1p(jnp.exp(-jnp.abs(x)))


def _dot(a, b):
    return jnp.dot(a, b, preferred_element_type=F32)


def _whole(memory_space=pltpu.VMEM):
    return pl.BlockSpec(memory_space=memory_space)


def _lane_chunk(c):
    return slice(c * LANES, (c + 1) * LANES)


def _mod_kernel(cond_ref, w_ref, b_ref, o_ref):
    c = cond_ref[...]
    s = (c * _sigmoid(c)).astype(BF16)
    o_ref[...] = _dot(s, w_ref[...].astype(BF16)) + b_ref[...]


def _modulation(cond, w_mod, b_mod):
    rows = cond.shape[0]
    n = w_mod.shape[1]
    bn = D
    return pl.pallas_call(
        _mod_kernel,
        grid=(n // bn,),
        in_specs=[
            pl.BlockSpec((rows, D), lambda j: (0, 0)),
            pl.BlockSpec((D, bn), lambda j: (0, j)),
            pl.BlockSpec((1, bn), lambda j: (0, j)),
        ],
        out_specs=pl.BlockSpec((rows, bn), lambda j: (0, j)),
        out_shape=jax.ShapeDtypeStruct((rows, n), F32),
        compiler_params=pltpu.CompilerParams(dimension_semantics=("arbitrary",)),
        name="modulation",
    )(cond, w_mod, b_mod.reshape(1, n))


def _ffn_interleaved(h, wg_ref, wu_ref, wd_ref, between):
    gate_up = lambda lo, hi: (_dot(h, wg_ref[:, lo:hi]), _dot(h, wu_ref[:, lo:hi]))
    down = None
    nxt = gate_up(*FF_CHUNKS[0])
    for q, (lo, hi) in enumerate(FF_CHUNKS):
        g, u = nxt
        if q + 1 < len(FF_CHUNKS):
            nxt = gate_up(*FF_CHUNKS[q + 1])
        act = ((g * _sigmoid(g)) * u).astype(BF16)
        part = _dot(act, wd_ref[lo:hi, :])
        down = part if down is None else down + part
        between(q)
    return down


def _scan_block(blk, reverse, tt, ext_scr, o_scr, carry, cw_ref, cb_ref, wg_ref, bg_ref, lam_ref):
    nb = SCAN_BATCH
    rows = tt * nb
    lo = blk * GATE_BLOCK
    lam_ls = _log_sigmoid(lam_ref[:, lo:lo + GATE_BLOCK])
    parts = []
    for half in range(GATE_HALVES):
        c = blk * GATE_HALVES + half
        cols = _lane_chunk(c)
        acc = ext_scr[c, 0:rows, :] * cw_ref[0:1, cols] + cb_ref[:, cols]
        for k in range(1, CONV_W):
            acc = acc + ext_scr[c, k * nb:k * nb + rows, :] * cw_ref[k:k + 1, cols]
        parts.append(acc)
    xr = jnp.concatenate(parts, axis=1)
    pre = _dot(xr.astype(BF16), wg_ref[blk]) + bg_ref[blk]
    r = _sigmoid(pre[:, 0:GATE_BLOCK])
    i = _sigmoid(pre[:, GATE_BLOCK:2 * GATE_BLOCK])
    log_a = (LRU_C * r) * lam_ls
    a = jnp.exp(log_a)
    mult = jnp.sqrt(jnp.maximum(-jnp.tanh(log_a) * (a * a + 1.0), 0.0))
    bx = mult * (i * xr)
    out = []
    for half in range(GATE_HALVES):
        c = blk * GATE_HALVES + half
        cs = _lane_chunk(half)
        h = carry[half]
        for t in (range(tt - 1, -1, -1) if reverse else range(tt)):
            rs = slice(t * nb, (t + 1) * nb)
            h = a[rs, cs] * h + bx[rs, cs]
            o_scr[c, rs, :] = h
        out.append(h)
    return out


def _scan_block_with_carry(blk, reverse, first, keep, h0_ref, carry_scr, ext_scr, o_scr, weights):
    chunks = range(blk * GATE_HALVES, (blk + 1) * GATE_HALVES)
    carry = [jnp.where(first, h0_ref[:, _lane_chunk(c)], carry_scr[c]) for c in chunks]
    new = _scan_block(blk, reverse, SCAN_T, ext_scr, o_scr, carry, *weights)
    kept = []
    for half, c in enumerate(chunks):
        v = new[half] if keep is None else jnp.where(keep, new[half], carry_scr[c])
        carry_scr[c] = v
        kept.append(v)
    return kept


def _unscramble(o_scr, dst):
    for c in range(N_LANE_CHUNKS):
        for s in range(SCAN_BATCH):
            dst(s, c, o_scr[c, pl.ds(s, SCAN_T, stride=SCAN_BATCH), :])


def _inproj_kernel(has_pe, nt, *refs):
    if has_pe:
        x_ref, pe_ref, refs = refs[0], refs[1], refs[2:]
    else:
        x_ref, pe_ref, refs = refs[0], None, refs[1:]
    (m_ref, h0_ref, n1_ref, n2_ref, gv_ref, wg_ref, wu_ref, wd_ref, win_ref,
     cw_ref, cb_ref, wgt_ref, bgt_ref, lam_ref,
     x1_ref, zx_ref, za_ref, hb_ref,
     ring_scr, ext_scr, o_scr, carry_scr) = refs
    k = pl.program_id(0)
    nb, tt, _ = x_ref.shape
    rows = nb * tt
    lead = CONV_LEFT * nb

    @pl.when(k == 0)
    def _():
        ring_scr[...] = jnp.zeros(ring_scr.shape, F32)
        carry_scr[...] = jnp.zeros(carry_scr.shape, F32)

    pos = jnp.maximum(k - IN_LAG, 0) % nt
    first = pos == 0
    slot = (k + IN_RING - IN_LAG) % IN_RING
    slot_early = (k + IN_RING - IN_LAG + 1) % IN_RING
    slot_late = (k + IN_RING - IN_LAG - 1) % IN_RING
    for c in range(N_LANE_CHUNKS):
        ext_scr[c, 0:lead, :] = jnp.where(
            pos < nt - 1, ring_scr[slot_early, c, rows - lead:rows, :], 0.0)
        ext_scr[c, lead:lead + rows, :] = ring_scr[slot, c]
        ext_scr[c, lead + rows:lead + rows + CONV_RIGHT * nb, :] = jnp.where(
            pos > 0, ring_scr[slot_late, c, 0:CONV_RIGHT * nb, :], 0.0)
    scan_weights = (cw_ref, cb_ref, wgt_ref, bgt_ref, lam_ref)

    def scan_block(blk):
        _scan_block_with_carry(blk, True, first, None, h0_ref, carry_scr, ext_scr, o_scr,
                               scan_weights)

    mv = lambda idx: m_ref[:, idx:idx + 1, :]
    x = x_ref[...]
    if has_pe:
        x = x + pe_ref[...][None]
    h = _modulate(x, n1_ref[...], mv(0), mv(1)).reshape(rows, D).astype(BF16)
    down = _ffn_interleaved(h, wg_ref, wu_ref, wd_ref, scan_block)

    def put_hb(s, c, v):
        hb_ref[s, :, _lane_chunk(c)] = v
    _unscramble(o_scr, put_hb)

    x1 = x + (0.5 * mv(2)) * down.reshape(nb, tt, D)
    x1_ref[...] = x1
    h2 = _modulate(x1, n2_ref[...], mv(3), mv(4)).reshape(rows, D).astype(BF16)
    zx = _dot(h2, win_ref[:, 0:D])
    zx_ref[...] = zx.reshape(nb, tt, D)
    slot_w = k % IN_RING
    for c in range(N_LANE_CHUNKS):
        for s in range(nb):
            ring_scr[slot_w, c, pl.ds(s, tt, stride=nb), :] = zx[s * tt:(s + 1) * tt, _lane_chunk(c)]

    branch_act = (jax.nn.gelu, jax.nn.gelu,
                  lambda v: _rms(jax.nn.gelu(v), gv_ref[...]), _sigmoid, _sigmoid)
    for j, fn in enumerate(branch_act):
        z = _dot(h2, win_ref[:, (j + 1) * D:(j + 2) * D])
        za_ref[:, :, j * D:(j + 1) * D] = fn(z).astype(BF16).reshape(nb, tt, D)


def _inproj(x, pe, m, h0, n1, n2, gv, wg, wu, wd, win, cw, cb, wgt, bgt, lam):
    B, T, _ = x.shape
    nb, tt = SCAN_BATCH, SCAN_T
    nt = T // tt
    n_stream = (B // nb) * nt
    has_pe = pe is not None
    per_seq_m = m.shape[0] > 1

    def proj_pos(k):
        kk = jnp.minimum(k, n_stream - 1)
        return kk // nt, nt - 1 - kk % nt

    def scan_pos(k):
        p = jnp.maximum(k - IN_LAG, 0)
        return p // nt, nt - 1 - p % nt

    tok_map = lambda k: proj_pos(k) + (0,)
    tok = lambda w: pl.BlockSpec((nb, tt, w), tok_map)
    in_specs = [tok(D)]
    args = [x]
    if has_pe:
        in_specs.append(pl.BlockSpec((tt, D), lambda k: (proj_pos(k)[1], 0)))
        args.append(pe)
    m_map = (lambda k: (proj_pos(k)[0], 0, 0)) if per_seq_m else (lambda k: (0, 0, 0))
    in_specs += [pl.BlockSpec((nb if per_seq_m else 1, N_MOD, D), m_map),
                 pl.BlockSpec((nb, D), lambda k: (scan_pos(k)[0], 0))] + [_whole()] * 12
    args += [m, h0, n1, n2, gv, wg, wu, wd, win, cw, cb, wgt, bgt, lam]
    return pl.pallas_call(
        functools.partial(_inproj_kernel, has_pe, nt),
        grid=(n_stream + IN_LAG,),
        in_specs=in_specs,
        out_specs=[tok(D), tok(D), tok(Z_ACT),
                   pl.BlockSpec((nb, tt, D), lambda k: scan_pos(k) + (0,))],
        out_shape=[jax.ShapeDtypeStruct((B, T, D), F32),
                   jax.ShapeDtypeStruct((B, T, D), F32),
                   jax.ShapeDtypeStruct((B, T, Z_ACT), BF16),
                   jax.ShapeDtypeStruct((B, T, D), F32)],
        scratch_shapes=[
            pltpu.VMEM((IN_RING, N_LANE_CHUNKS, tt * nb, LANES), F32),
            pltpu.VMEM((N_LANE_CHUNKS, (tt + CONV_W - 1) * nb, LANES), F32),
            pltpu.VMEM((N_LANE_CHUNKS, tt * nb, LANES), F32),
            pltpu.VMEM((N_LANE_CHUNKS, nb, LANES), F32),
        ],
        compiler_params=pltpu.CompilerParams(
            dimension_semantics=("arbitrary",),
            vmem_limit_bytes=VMEM_LIMIT),
        name="inproj_bwdscan",
    )(*args)


def _fill_ext(ext_scr, zc_ref, zl_ref, zrt_ref, has_left, has_right):
    nb, tt, _ = zc_ref.shape
    rows = tt * nb
    lead = CONV_LEFT * nb
    for c in range(N_LANE_CHUNKS):
        cols = _lane_chunk(c)
        for k in range(CONV_LEFT):
            ext_scr[c, k * nb:(k + 1) * nb, :] = jnp.where(
                has_left, zl_ref[:, SUBLANES - CONV_LEFT + k, cols], 0.0)
        for k in range(CONV_RIGHT):
            ext_scr[c, lead + rows + k * nb:lead + rows + (k + 1) * nb, :] = jnp.where(
                has_right, zrt_ref[:, k, cols], 0.0)
        for s in range(nb):
            ext_scr[c, pl.ds(lead + s, tt, stride=nb), :] = zc_ref[s, :, cols]


def _mix_kernel(nt, n_stream,
                zc_ref, zl_ref, zrt_ref, h0_ref, cw_ref, cb_ref, wgt_ref, bgt_ref, lam_ref,
                x1_ref, hb_ref, za_ref, m_ref, ws_ref, bs_ref,
                wbr_ref, wbg_ref, wo_ref, n3_ref, wg_ref, wu_ref, wd_ref, nf_ref,
                y_ref, hlast_ref,
                ext_scr, o_scr, carry_scr, hfwin_scr, mixed_scr):
    k = pl.program_id(0)

    @pl.when(k == 0)
    def _():
        hfwin_scr[1] = jnp.zeros(hfwin_scr.shape[1:], F32)
        carry_scr[...] = jnp.zeros(carry_scr.shape, F32)

    bb, tt, _ = x1_ref.shape
    rows = bb * tt
    u_m = jnp.maximum(k - MIX_LAG, 0) % MIX_LAG
    slot_m = (k // MIX_LAG + 1) % 2
    hf = hfwin_scr[slot_m, pl.ds(u_m * MIX_SEQS, MIX_SEQS)]
    mv = lambda idx: m_ref[:, idx:idx + 1, :]
    zcol = lambda c: za_ref[:, :, c * D:(c + 1) * D].reshape(rows, D)

    y_rnn = zcol(0).astype(F32) * (hf + hb_ref[...]).reshape(rows, D)
    rnn_proj = _dot(y_rnn.astype(BF16), wbr_ref[...])

    vn = zcol(2)
    for n in range(rows // CHUNK):
        rs = slice(n * CHUNK, (n + 1) * CHUNK)
        for g in range(GROUPS):
            cs = _lane_chunk(g)
            mixed_scr[rs, cs] = _dot(ws_ref[g], vn[rs, cs]) + bs_ref[:, cs]
    y_g = zcol(1).astype(F32) * mixed_scr[...]
    g_proj = _dot(y_g.astype(BF16), wbg_ref[...])

    merged = zcol(3).astype(F32) * rnn_proj + zcol(4).astype(F32) * g_proj
    x2 = x1_ref[...] + mv(5) * _dot(merged.astype(BF16), wo_ref[...]).reshape(bb, tt, D)

    valid = k < n_stream
    ts = jnp.minimum(k, n_stream - 1) % nt
    _fill_ext(ext_scr, zc_ref, zl_ref, zrt_ref, ts > 0, ts < nt - 1)
    scan_weights = (cw_ref, cb_ref, wgt_ref, bgt_ref, lam_ref)

    def scan_block(blk):
        kept = _scan_block_with_carry(blk, False, ts == 0, valid, h0_ref, carry_scr,
                                      ext_scr, o_scr, scan_weights)
        for half, v in enumerate(kept):
            hlast_ref[:, _lane_chunk(blk * GATE_HALVES + half)] = v

    h = _modulate(x2, n3_ref[...], mv(6), mv(7)).reshape(rows, D).astype(BF16)
    down = _ffn_interleaved(h, wg_ref, wu_ref, wd_ref, scan_block)
    x3 = x2 + (0.5 * mv(8)) * down.reshape(bb, tt, D)
    y_ref[...] = _rms(x3, nf_ref[...])

    slot_s = (k // MIX_LAG) % 2
    t0 = pl.multiple_of((k % MIX_LAG) * SCAN_T, SUBLANES)

    def put_hf(s, c, v):
        hfwin_scr[slot_s, s, pl.ds(t0, SCAN_T), _lane_chunk(c)] = v
    _unscramble(o_scr, put_hf)


def _mix(zx, h0, hb, x1, za, m, cw, cb, wgt, bgt, lam, ws, bs, wbr, wbg, wo, n3, wg, wu, wd, nf):
    B, T, _ = x1.shape
    nb = SCAN_BATCH
    nt = T // SCAN_T
    n_stream = (B // nb) * nt
    tpb = SCAN_T // SUBLANES
    nhb = T // SUBLANES
    per_seq_m = m.shape[0] > 1

    def scan_pos(k):
        kk = jnp.minimum(k, n_stream - 1)
        return kk // nt, kk % nt

    def mix_pos(k):
        km = jnp.maximum(k - MIX_LAG, 0)
        return (km // nt) * (nb // MIX_SEQS) + km % MIX_LAG, (km % nt) // MIX_LAG

    def zc_map(k):
        g, ts = scan_pos(k)
        return g, ts, 0

    def zl_map(k):
        g, ts = scan_pos(k)
        return g, jnp.maximum(ts * tpb - 1, 0), 0

    def zrt_map(k):
        g, ts = scan_pos(k)
        return g, jnp.minimum((ts + 1) * tpb, nhb - 1), 0

    grp_map = lambda k: (scan_pos(k)[0], 0)
    tok_map = lambda k: mix_pos(k) + (0,)
    m_map = (lambda k: (mix_pos(k)[0], 0, 0)) if per_seq_m else (lambda k: (0, 0, 0))
    tok = lambda w: pl.BlockSpec((MIX_SEQS, MIX_T, w), tok_map)
    return pl.pallas_call(
        functools.partial(_mix_kernel, nt, n_stream),
        grid=(n_stream + MIX_LAG,),
        in_specs=[
            pl.BlockSpec((nb, SCAN_T, D), zc_map),
            pl.BlockSpec((nb, SUBLANES, D), zl_map),
            pl.BlockSpec((nb, SUBLANES, D), zrt_map),
            pl.BlockSpec((nb, D), grp_map),
            _whole(), _whole(), _whole(), _whole(), _whole(),
            tok(D), tok(D), tok(Z_ACT),
            pl.BlockSpec((MIX_SEQS if per_seq_m else 1, N_MOD, D), m_map),
        ] + [_whole()] * 10,
        out_specs=[tok(D), pl.BlockSpec((nb, D), grp_map)],
        out_shape=[jax.ShapeDtypeStruct((B, T, D), F32),
                   jax.ShapeDtypeStruct((B, D), F32)],
        scratch_shapes=[
            pltpu.VMEM((N_LANE_CHUNKS, (SCAN_T + CONV_W - 1) * nb, LANES), F32),
            pltpu.VMEM((N_LANE_CHUNKS, SCAN_T * nb, LANES), F32),
            pltpu.VMEM((N_LANE_CHUNKS, nb, LANES), F32),
            pltpu.VMEM((2, nb, MIX_T, D), F32),
            pltpu.VMEM((MIX_SEQS * MIX_T, D), F32),
        ],
        compiler_params=pltpu.CompilerParams(
            dimension_semantics=("arbitrary",),
            vmem_limit_bytes=VMEM_LIMIT),
        name="mix_fwdscan",
    )(zx, zx, zx, h0, cw, cb, wgt, bgt, lam,
      x1, hb, za, m, ws, bs, wbr, wbg, wo, n3, wg, wu, wd, nf)


def _grid_pos_embed(n_tokens):
    rows = n_tokens // GRID_W
    t = jnp.arange(rows * GRID_W)
    r = (t // GRID_W).astype(F32)
    col = (t % GRID_W).astype(F32)
    q = D // 4
    freqs = 1.0 / (10000.0 ** (jnp.arange(q, dtype=F32) / q))
    ang_r = r[:, None] * freqs
    ang_c = col[:, None] * freqs
    return jnp.concatenate([jnp.sin(ang_r), jnp.cos(ang_r), jnp.sin(ang_c), jnp.cos(ang_c)], axis=-1)


def _gate_weights(w_r, w_i, b_r, b_i):
    def bd(w):
        w4 = w.reshape(N_GATE_BLOCKS, HEADS_PER_BLOCK, HEAD_DIM, HEAD_DIM)
        eye = jnp.eye(HEADS_PER_BLOCK, dtype=w.dtype)
        return jnp.einsum('ghij,hk->ghikj', w4, eye).reshape(N_GATE_BLOCKS, GATE_BLOCK, GATE_BLOCK)
    wg = jnp.concatenate([bd(w_r), bd(w_i)], axis=-1).astype(BF16)
    bg = jnp.concatenate([b_r.reshape(N_GATE_BLOCKS, 1, GATE_BLOCK),
                          b_i.reshape(N_GATE_BLOCKS, 1, GATE_BLOCK)], axis=-1)
    return wg, bg


def kernel(x_prompt, x_sample, state_rnn_fwd, state_rnn_bwd, c, c_ctx, w_mod, b_mod, norm1, norm2, norm3, ff1_gate, ff1_up, ff1_down, w_in, conv_w, conv_b, w_r, b_r, w_i, b_i, lam, gmlp_norm, w_s, b_s, w_br, w_bg, w_out, ff2_gate, ff2_up, ff2_down, norm_f):
    l = 0
    nbatch_lat = c.shape[0]
    cond_rows = 2 * SUBLANES
    cond = jnp.zeros((cond_rows, D), F32).at[:nbatch_lat].set(c).at[nbatch_lat].set(c_ctx)
    m_all = _modulation(cond, w_mod[l], b_mod[l])
    m_lat = m_all[:nbatch_lat].reshape(nbatch_lat, N_MOD, D)
    m_ctx = m_all[nbatch_lat:nbatch_lat + 1].reshape(1, N_MOD, D)

    row = lambda v: v.reshape(1, D)
    bf = lambda w: w.astype(BF16)
    n1, n2, n3, nf, gv = row(norm1[l]), row(norm2[l]), row(norm3[l]), row(norm_f), row(gmlp_norm[l])
    wg1, wu1, wd1 = bf(ff1_gate[l]), bf(ff1_up[l]), bf(ff1_down[l])
    wg2, wu2, wd2 = bf(ff2_gate[l]), bf(ff2_up[l]), bf(ff2_down[l])
    win, wbr, wbg, wo = bf(w_in[l]), bf(w_br[l]), bf(w_bg[l]), bf(w_out[l])
    ws = bf(w_s[l])
    bs = jnp.repeat(b_s[l].T, D // GROUPS, axis=1)
    cw, cb, lam_l = conv_w[l], row(conv_b[l]), lam[l]
    (wgt_f, bgt_f), (wgt_b, bgt_b) = [
        _gate_weights(w_r[l, d], w_i[l, d], b_r[l, d], b_i[l, d]) for d in range(2)]

    def trunk(x, pe, m, h0_f, h0_b):
        x1, zx, za, hb = _inproj(x, pe, m, h0_b, n1, n2, gv, wg1, wu1, wd1, win,
                                 cw, cb, wgt_b, bgt_b, row(lam_l[1]))
        y, hf_last = _mix(zx, h0_f, hb, x1, za, m, cw, cb, wgt_f, bgt_f, row(lam_l[0]),
                          ws, bs, wbr, wbg, wo, n3, wg2, wu2, wd2, nf)
        return y, hf_last, hb

    zeros = jnp.zeros((x_prompt.shape[0], D), F32)
    y_prompt, hf_last, hb_c = trunk(x_prompt, None, m_ctx, zeros, zeros)
    new_f = hf_last[:, None, :]
    new_b = hb_c[:, 0:1, :]

    pe = _grid_pos_embed(x_sample.shape[1])
    y_sample, _, _ = trunk(x_sample, pe, m_lat, state_rnn_fwd[:, l], state_rnn_bwd[:, l])
    return (y_prompt, y_sample, new_f, new_b)
```

```python
import functools

import jax
import jax.numpy as jnp
from jax import lax
from jax.experimental import pallas as pl
from jax.experimental.pallas import tpu as pltpu

D = 1024
D_FF = 2816
N_MOD = 9
EPS = 1e-6
CHUNK = 128
GROUPS = 8
HEADS = 16
HEAD_DIM = 64
CONV_W = 4
CONV_LEFT = 2
CONV_RIGHT = CONV_W - 1 - CONV_LEFT
LRU_C = 8.0
GRID_W = 64
N_BRANCH = 5
Z_ACT = N_BRANCH * D
ZA_MIX = 4 * D
ZA_RNN_GATE = 4

LANES = 128
SUBLANES = 8
N_LANE_CHUNKS = D // LANES
HEADS_PER_BLOCK = 4
GATE_BLOCK = HEADS_PER_BLOCK * HEAD_DIM
N_GATE_BLOCKS = D // GATE_BLOCK
GATE_HALVES = GATE_BLOCK // LANES

STAGE1_T = 256
STAGE3_T = 512
SCAN_BATCH = SUBLANES
SCAN_T = 128
VMEM_LIMIT = 56 * 1024 * 1024

F32 = jnp.float32
BF16 = jnp.bfloat16


def _rms(x, g):
    ms = jnp.mean(x * x, axis=-1, keepdims=True)
    return (x * lax.rsqrt(ms + EPS)) * g


def _modulate(x, g, shift, scale):
    return _rms(x, g) * (1.0 + scale) + shift


def _sigmoid(x):
    return 0.5 * jnp.tanh(0.5 * x) + 0.5


def _log_sigmoid(x):
    return jnp.minimum(x, 0.0) - jnp.log1p(jnp.exp(-jnp.abs(x)))


def _dot(a, b):
    return jnp.dot(a, b, preferred_element_type=F32)


def _whole(memory_space=pltpu.VMEM):
    return pl.BlockSpec(memory_space=memory_space)


def _lane_chunk(c):
    return slice(c * LANES, (c + 1) * LANES)


def _mod_kernel(cond_ref, w_ref, b_ref, o_ref):
    c = cond_ref[...]
    s = (c * _sigmoid(c)).astype(BF16)
    o_ref[...] = _dot(s, w_ref[...].astype(BF16)) + b_ref[...]


def _modulation(cond, w_mod, b_mod):
    rows = cond.shape[0]
    n = w_mod.shape[1]
    bn = D
    return pl.pallas_call(
        _mod_kernel,
        grid=(n // bn,),
        in_specs=[
            pl.BlockSpec((rows, D), lambda j: (0, 0)),
            pl.BlockSpec((D, bn), lambda j: (0, j)),
            pl.BlockSpec((1, bn), lambda j: (0, j)),
        ],
        out_specs=pl.BlockSpec((rows, bn), lambda j: (0, j)),
        out_shape=jax.ShapeDtypeStruct((rows, n), F32),
        compiler_params=pltpu.CompilerParams(dimension_semantics=("arbitrary",)),
        name="modulation",
    )(cond, w_mod, b_mod.reshape(1, n))


def _stage1_kernel(has_pe, *refs):
    if has_pe:
        x_ref, per_ref, pec_ref, refs = refs[0], refs[1], refs[2], refs[3:]
    else:
        x_ref, refs = refs[0], refs[1:]
    (m_ref, n1_ref, n2_ref, gv_ref, wg_ref, wu_ref, wd_ref, win_ref,
     x1_ref, zx_ref, za_ref) = refs
    tt = x_ref.shape[1]
    x = x_ref[0]
    if has_pe:
        reps = GRID_W // SUBLANES
        row_half = jnp.concatenate(
            [jnp.tile(per_ref[r * SUBLANES:(r + 1) * SUBLANES, :], (reps, 1))
             for r in range(tt // GRID_W)], axis=0)
        col_half = jnp.tile(pec_ref[...], (tt // GRID_W, 1))
        x = x + jnp.concatenate([row_half, col_half], axis=1)
    m = m_ref[0]
    h = _modulate(x, n1_ref[...], m[0:1], m[1:2]).astype(BF16)
    g = _dot(h, wg_ref[...])
    u = _dot(h, wu_ref[...])
    act = ((g * _sigmoid(g)) * u).astype(BF16)
    x1 = x + (0.5 * m[2:3]) * _dot(act, wd_ref[...])
    x1_ref[0] = x1
    h2 = _modulate(x1, n2_ref[...], m[3:4], m[4:5]).astype(BF16)
    branches = ((2, jax.nn.gelu), (3, lambda v: _rms(jax.nn.gelu(v), gv_ref[...])),
                (4, _sigmoid), (5, _sigmoid), (1, jax.nn.gelu))
    for j, (src, fn) in enumerate(branches):
        z = _dot(h2, win_ref[:, src * D:(src + 1) * D])
        za_ref[0, :, j * D:(j + 1) * D] = fn(z).astype(BF16)
    zx_ref[0] = _dot(h2, win_ref[:, 0:D])


def _stage1(x, pe_tables, m, n1, n2, gv, wg, wu, wd, win):
    B, T, _ = x.shape
    tt = min(T, STAGE1_T)
    has_pe = pe_tables is not None
    per_batch_m = m.shape[0] > 1
    m_map = (lambda t, b: (b, 0, 0)) if per_batch_m else (lambda t, b: (0, 0, 0))
    in_specs = [pl.BlockSpec((1, tt, D), lambda t, b: (b, t, 0))]
    args = [x]
    if has_pe:
        pe_rows, pe_cols = pe_tables
        in_specs += [pl.BlockSpec((tt // GRID_W * SUBLANES, D // 2), lambda t, b: (t, 0)), _whole()]
        args += [pe_rows, pe_cols]
    in_specs += [pl.BlockSpec((1, N_MOD, D), m_map)] + [_whole()] * 7
    args += [m, n1, n2, gv, wg, wu, wd, win]
    blk = lambda w: pl.BlockSpec((1, tt, w), lambda t, b: (b, t, 0))
    return pl.pallas_call(
        functools.partial(_stage1_kernel, has_pe),
        grid=(T // tt, B),
        in_specs=in_specs,
        out_specs=[blk(D), blk(D), blk(Z_ACT)],
        out_shape=[jax.ShapeDtypeStruct((B, T, D), F32),
                   jax.ShapeDtypeStruct((B, T, D), F32),
                   jax.ShapeDtypeStruct((B, T, Z_ACT), BF16)],
        compiler_params=pltpu.CompilerParams(
            dimension_semantics=("arbitrary", "arbitrary"),
            vmem_limit_bytes=VMEM_LIMIT),
        name="stage1_ffn_inproj",
    )(*args)


def _gates_to_scratch(blk, xr, wg_ref, bg_ref, lam_ref, a_scr, b_scr):
    lo = blk * GATE_BLOCK
    c4 = (0.5 * LRU_C) * _log_sigmoid(lam_ref[:, lo:lo + GATE_BLOCK])
    t = jnp.tanh(_dot(xr.astype(BF16), wg_ref[blk]) + bg_ref[blk])
    t_r = t[:, 0:GATE_BLOCK]
    t_i = t[:, GATE_BLOCK:2 * GATE_BLOCK]
    log_a = t_r * c4 + c4
    a = jnp.exp(log_a)
    half_mult = jnp.sqrt(jnp.maximum((-0.25 * jnp.tanh(log_a)) * (a * a + 1.0), 0.0))
    w = half_mult * xr
    bx = w * t_i + w
    for half in range(GATE_HALVES):
        c = blk * GATE_HALVES + half
        a_scr[c] = a[:, _lane_chunk(half)]
        b_scr[c] = bx[:, _lane_chunk(half)]


def _recurrence(reverse, tt, first, h0_ref, put, a_scr, b_scr, o_scr, carry_scr):
    nb = SCAN_BATCH

    @pl.when(first)
    def _():
        for c in range(N_LANE_CHUNKS):
            carry_scr[c] = h0_ref[:, _lane_chunk(c)]

    def step(s, hs):
        t = (tt - 1 - s) if reverse else s
        r0 = pl.multiple_of(t * nb, SUBLANES)
        out = []
        for c in range(N_LANE_CHUNKS):
            h = a_scr[c, pl.ds(r0, nb), :] * hs[c] + b_scr[c, pl.ds(r0, nb), :]
            o_scr[c, pl.ds(r0, nb), :] = h
            out.append(h)
        return tuple(out)

    hs = lax.fori_loop(0, tt, step,
                       tuple(carry_scr[c] for c in range(N_LANE_CHUNKS)), unroll=8)
    for c in range(N_LANE_CHUNKS):
        carry_scr[c] = hs[c]
        for s in range(nb):
            put(s, c, o_scr[c, pl.ds(s, tt, stride=nb), :])
    return hs


def _scan_bwd_kernel(zc_ref, zl_ref, zrt_ref, h0_ref, cw_ref, cb_ref, wg_ref, bg_ref, lam_ref,
                     h_ref, xr_ref, ext_scr, a_scr, b_scr, o_scr, carry_scr):
    nb, tt, _ = zc_ref.shape
    j = pl.program_id(1)
    nt = pl.num_programs(1)
    ti = nt - 1 - j
    rows = tt * nb
    lead = CONV_LEFT * nb

    for c in range(N_LANE_CHUNKS):
        cols = _lane_chunk(c)
        for k in range(CONV_LEFT):
            ext_scr[c, k * nb:(k + 1) * nb, :] = jnp.where(
                ti > 0, zl_ref[:, SUBLANES - CONV_LEFT + k, cols], 0.0)
        for k in range(CONV_RIGHT):
            ext_scr[c, lead + rows + k * nb:lead + rows + (k + 1) * nb, :] = jnp.where(
                ti < nt - 1, zrt_ref[:, k, cols], 0.0)
        for s in range(nb):
            ext_scr[c, pl.ds(lead + s, tt, stride=nb), :] = zc_ref[s, :, cols]

    for blk in range(N_GATE_BLOCKS):
        parts = []
        for half in range(GATE_HALVES):
            c = blk * GATE_HALVES + half
            cols = _lane_chunk(c)
            acc = ext_scr[c, 0:rows, :] * cw_ref[0:1, cols] + cb_ref[:, cols]
            for k in range(1, CONV_W):
                acc = acc + ext_scr[c, k * nb:k * nb + rows, :] * cw_ref[k:k + 1, cols]
            parts.append(acc)
        xr = jnp.concatenate(parts, axis=1)
        xr_ref[0, :, blk * GATE_BLOCK:(blk + 1) * GATE_BLOCK] = xr
        _gates_to_scratch(blk, xr, wg_ref, bg_ref, lam_ref, a_scr, b_scr)

    def put(s, c, h):
        h_ref[s, :, _lane_chunk(c)] = h
    _recurrence(True, tt, j == 0, h0_ref, put, a_scr, b_scr, o_scr, carry_scr)


def _scan_fwd_kernel(xr_ref, h0_ref, hb_ref, gr_ref, wg_ref, bg_ref, lam_ref,
                     yrnn_ref, hlast_ref, a_scr, b_scr, o_scr, carry_scr):
    nb, tt, _ = hb_ref.shape
    for blk in range(N_GATE_BLOCKS):
        xr = xr_ref[0, :, blk * GATE_BLOCK:(blk + 1) * GATE_BLOCK]
        _gates_to_scratch(blk, xr, wg_ref, bg_ref, lam_ref, a_scr, b_scr)

    def put(s, c, hf):
        cols = _lane_chunk(c)
        yrnn_ref[s, :, cols] = (gr_ref[s, :, cols].astype(F32) * (hf + hb_ref[s, :, cols])).astype(BF16)
    hs = _recurrence(False, tt, pl.program_id(1) == 0, h0_ref, put, a_scr, b_scr, o_scr, carry_scr)
    for c in range(N_LANE_CHUNKS):
        hlast_ref[:, _lane_chunk(c)] = hs[c]


def _scan_scratch(nb, tt):
    tile = pltpu.VMEM((N_LANE_CHUNKS, tt * nb, LANES), F32)
    return [tile, tile, tile, pltpu.VMEM((N_LANE_CHUNKS, nb, LANES), F32)]


def _scan_bwd(zx, h0, conv_w, conv_b, wgate, bgate, lam):
    B, T, _ = zx.shape
    nb = SCAN_BATCH
    tt = min(T, SCAN_T)
    nt = T // tt
    tpb = tt // SUBLANES
    nhb = T // SUBLANES
    ti = lambda j: nt - 1 - j
    return pl.pallas_call(
        _scan_bwd_kernel,
        grid=(B // nb, nt),
        in_specs=[
            pl.BlockSpec((nb, tt, D), lambda g, j: (g, ti(j), 0)),
            pl.BlockSpec((nb, SUBLANES, D),
                         lambda g, j: (g, jnp.maximum(ti(j) * tpb - 1, 0), 0)),
            pl.BlockSpec((nb, SUBLANES, D),
                         lambda g, j: (g, jnp.minimum((ti(j) + 1) * tpb, nhb - 1), 0)),
            pl.BlockSpec((nb, D), lambda g, j: (g, 0)),
            _whole(), _whole(), _whole(), _whole(), _whole(),
        ],
        out_specs=[pl.BlockSpec((nb, tt, D), lambda g, j: (g, ti(j), 0)),
                   pl.BlockSpec((1, tt * nb, D), lambda g, j: (g, ti(j), 0))],
        out_shape=[jax.ShapeDtypeStruct((B, T, D), F32),
                   jax.ShapeDtypeStruct((B // nb, T * nb, D), F32)],
        scratch_shapes=[pltpu.VMEM((N_LANE_CHUNKS, (tt + CONV_W - 1) * nb, LANES), F32)]
        + _scan_scratch(nb, tt),
        compiler_params=pltpu.CompilerParams(
            dimension_semantics=("arbitrary", "arbitrary"),
            vmem_limit_bytes=VMEM_LIMIT),
        name="scan_bwd",
    )(zx, zx, zx, h0, conv_w, conv_b, wgate, bgate, lam)


def _scan_fwd(xr_tm, h0, hb, za, wgate, bgate, lam):
    B, T, _ = hb.shape
    nb = SCAN_BATCH
    tt = min(T, SCAN_T)
    tok = pl.BlockSpec((nb, tt, D), lambda g, j: (g, j, 0))
    return pl.pallas_call(
        _scan_fwd_kernel,
        grid=(B // nb, T // tt),
        in_specs=[
            pl.BlockSpec((1, tt * nb, D), lambda g, j: (g, j, 0)),
            pl.BlockSpec((nb, D), lambda g, j: (g, 0)),
            tok,
            pl.BlockSpec((nb, tt, D), lambda g, j: (g, j, ZA_RNN_GATE)),
            _whole(), _whole(), _whole(),
        ],
        out_specs=[tok, pl.BlockSpec((nb, D), lambda g, j: (g, 0))],
        out_shape=[jax.ShapeDtypeStruct((B, T, D), BF16),
                   jax.ShapeDtypeStruct((B, D), F32)],
        scratch_shapes=_scan_scratch(nb, tt),
        compiler_params=pltpu.CompilerParams(
            dimension_semantics=("arbitrary", "arbitrary"),
            vmem_limit_bytes=VMEM_LIMIT),
        name="scan_fwd",
    )(xr_tm, h0, hb, za, wgate, bgate, lam)


def _stage3_kernel(x1_ref, yrnn_ref, za_ref, m_ref, ws_ref, bs_ref,
                   wbr_ref, wbg_ref, wo_ref, n3_ref, wg_ref, wu_ref, wd_ref, nf_ref,
                   y_ref, mixed_scr):
    rows = x1_ref.shape[1]
    m = m_ref[0]
    zcol = lambda c: za_ref[0, :, c * D:(c + 1) * D]

    rnn_proj = _dot(yrnn_ref[0], wbr_ref[...])

    vn = zcol(1)
    for n in range(rows // CHUNK):
        rs = slice(n * CHUNK, (n + 1) * CHUNK)
        for g in range(GROUPS):
            cs = _lane_chunk(g)
            mixed_scr[rs, cs] = _dot(ws_ref[g], vn[rs, cs]) + bs_ref[:, cs]
    y_g = zcol(0).astype(F32) * mixed_scr[...]
    g_proj = _dot(y_g.astype(BF16), wbg_ref[...])

    merged = zcol(2).astype(F32) * rnn_proj + zcol(3).astype(F32) * g_proj
    x2 = x1_ref[0] + m[5:6] * _dot(merged.astype(BF16), wo_ref[...])

    h = _modulate(x2, n3_ref[...], m[6:7], m[7:8]).astype(BF16)
    g = _dot(h, wg_ref[...])
    uu = _dot(h, wu_ref[...])
    act = ((g * _sigmoid(g)) * uu).astype(BF16)
    x3 = x2 + (0.5 * m[8:9]) * _dot(act, wd_ref[...])
    y_ref[0] = _rms(x3, nf_ref[...])


def _stage3(x1, y_rnn, za, m, ws, bs, wbr, wbg, wo, n3, wg, wu, wd, nf):
    B, T, _ = x1.shape
    tt = min(T, STAGE3_T)
    per_batch_m = m.shape[0] > 1
    m_map = (lambda t, b: (b, 0, 0)) if per_batch_m else (lambda t, b: (0, 0, 0))
    blk = lambda w: pl.BlockSpec((1, tt, w), lambda t, b: (b, t, 0))
    return pl.pallas_call(
        _stage3_kernel,
        grid=(T // tt, B),
        in_specs=[blk(D), blk(D), blk(ZA_MIX),
                  pl.BlockSpec((1, N_MOD, D), m_map)] + [_whole()] * 10,
        out_specs=blk(D),
        out_shape=jax.ShapeDtypeStruct((B, T, D), F32),
        scratch_shapes=[pltpu.VMEM((tt, D), F32)],
        compiler_params=pltpu.CompilerParams(
            dimension_semantics=("arbitrary", "arbitrary"),
            vmem_limit_bytes=VMEM_LIMIT),
        name="stage3_mix_ffn",
    )(x1, y_rnn, za, m, ws, bs, wbr, wbg, wo, n3, wg, wu, wd, nf)


def _grid_pos_tables(n_tokens):
    q = D // 4
    freqs = 1.0 / (10000.0 ** (jnp.arange(q, dtype=F32) / q))
    half = lambda idx: jnp.concatenate(
        [jnp.sin(idx[:, None] * freqs), jnp.cos(idx[:, None] * freqs)], axis=-1)
    rows = half(jnp.arange(n_tokens // GRID_W).astype(F32))
    cols = half(jnp.arange(GRID_W).astype(F32))
    return jnp.repeat(rows, SUBLANES, axis=0), cols


def _gate_weights(w_r, w_i, b_r, b_i):
    def bd(w):
        w4 = w.reshape(N_GATE_BLOCKS, HEADS_PER_BLOCK, HEAD_DIM, HEAD_DIM)
        eye = jnp.eye(HEADS_PER_BLOCK, dtype=w.dtype)
        return jnp.einsum('ghij,hk->ghikj', w4, eye).reshape(N_GATE_BLOCKS, GATE_BLOCK, GATE_BLOCK)
    wg = (0.5 * jnp.concatenate([bd(w_r), bd(w_i)], axis=-1)).astype(BF16)
    bg = 0.5 * jnp.concatenate([b_r.reshape(N_GATE_BLOCKS, 1, GATE_BLOCK),
                                b_i.reshape(N_GATE_BLOCKS, 1, GATE_BLOCK)], axis=-1)
    return wg, bg


def kernel(x_prompt, x_sample, state_rnn_fwd, state_rnn_bwd, c, c_ctx, w_mod, b_mod, norm1, norm2, norm3, ff1_gate, ff1_up, ff1_down, w_in, conv_w, conv_b, w_r, b_r, w_i, b_i, lam, gmlp_norm, w_s, b_s, w_br, w_bg, w_out, ff2_gate, ff2_up, ff2_down, norm_f):
    l = 0
    nbatch_lat = c.shape[0]
    cond_rows = 2 * SUBLANES
    cond = jnp.zeros((cond_rows, D), F32).at[:nbatch_lat].set(c).at[nbatch_lat].set(c_ctx)
    m_all = _modulation(cond, w_mod[l], b_mod[l])
    m_lat = m_all[:nbatch_lat].reshape(nbatch_lat, N_MOD, D)
    m_ctx = m_all[nbatch_lat:nbatch_lat + 1].reshape(1, N_MOD, D)

    row = lambda v: v.reshape(1, D)
    bf = lambda w: w.astype(BF16)
    n1, n2, n3, nf, gv = row(norm1[l]), row(norm2[l]), row(norm3[l]), row(norm_f), row(gmlp_norm[l])
    wg1, wu1, wd1 = bf(ff1_gate[l]), bf(ff1_up[l]), bf(ff1_down[l])
    wg2, wu2, wd2 = bf(ff2_gate[l]), bf(ff2_up[l]), bf(ff2_down[l])
    win, wbr, wbg, wo = bf(w_in[l]), bf(w_br[l]), bf(w_bg[l]), bf(w_out[l])
    ws = bf(w_s[l])
    bs = jnp.repeat(b_s[l].T, D // GROUPS, axis=1)
    cw, cb, lam_l = conv_w[l], row(conv_b[l]), lam[l]
    (wgt_f, bgt_f), (wgt_b, bgt_b) = [
        _gate_weights(w_r[l, d], w_i[l, d], b_r[l, d], b_i[l, d]) for d in range(2)]

    def trunk(x, pe_tables, m, h0_f, h0_b):
        x1, zx, za = _stage1(x, pe_tables, m, n1, n2, gv, wg1, wu1, wd1, win)
        hb, xr_tm = _scan_bwd(zx, h0_b, cw, cb, wgt_b, bgt_b, row(lam_l[1]))
        y_rnn, hf_last = _scan_fwd(xr_tm, h0_f, hb, za, wgt_f, bgt_f, row(lam_l[0]))
        y = _stage3(x1, y_rnn, za, m, ws, bs, wbr, wbg, wo, n3, wg2, wu2, wd2, nf)
        return y, hf_last, hb

    zeros = jnp.zeros((x_prompt.shape[0], D), F32)
    y_prompt, hf_last, hb_c = trunk(x_prompt, None, m_ctx, zeros, zeros)
    new_f = hf_last[:, None, :]
    new_b = hb_c[:, 0:1, :]

    y_sample, _, _ = trunk(x_sample, _grid_pos_tables(x_sample.shape[1]), m_lat,
                           state_rnn_fwd[:, l], state_rnn_bwd[:, l])
    return (y_prompt, y_sample, new_f, new_b)
```

```python
import functools

import jax
import jax.numpy as jnp
from jax import lax
from jax.experimental import pallas as pl
from jax.experimental.pallas import tpu as pltpu

D = 1024
D_FF = 2816
N_MOD = 9
EPS = 1e-6
CHUNK = 128
GROUPS = 8
HEADS = 16
HEAD_DIM = 64
CONV_W = 4
CONV_LEFT = 2
CONV_RIGHT = CONV_W - 1 - CONV_LEFT
LRU_C = 8.0
GRID_W = 64
N_BRANCH = 5
Z_ACT = N_BRANCH * D
ZA_MIX = 4 * D
ZA_RNN_GATE = 4

LANES = 128
SUBLANES = 8
N_LANE_CHUNKS = D // LANES
HEADS_PER_BLOCK = 4
GATE_BLOCK = HEADS_PER_BLOCK * HEAD_DIM
N_GATE_BLOCKS = D // GATE_BLOCK
GATE_HALVES = GATE_BLOCK // LANES

SUB_T = 256
STAGE1_T = 512
STAGE3_T = 512
SCAN_BATCH = SUBLANES
SCAN_T = 128
VMEM_LIMIT = 61 * 1024 * 1024

F32 = jnp.float32
BF16 = jnp.bfloat16


def _rms(x, g):
    ms = jnp.mean(x * x, axis=-1, keepdims=True)
    return (x * lax.rsqrt(ms + EPS)) * g


def _modulate(x, g, shift, scale):
    return _rms(x, g) * (1.0 + scale) + shift


def _sigmoid(x):
    return 0.5 * jnp.tanh(0.5 * x) + 0.5


def _log_sigmoid(x):
    return jnp.minimum(x, 0.0) - jnp.log1p(jnp.exp(-jnp.abs(x)))


def _dot(a, b):
    return jnp.dot(a, b, preferred_element_type=F32)


def _whole(memory_space=pltpu.VMEM):
    return pl.BlockSpec(memory_space=memory_space)


def _lane_chunk(c):
    return slice(c * LANES, (c + 1) * LANES)


def _subtiles(ref):
    bb, tt = ref.shape[0], ref.shape[1]
    return [(b, slice(s * SUB_T, (s + 1) * SUB_T)) for b in range(bb) for s in range(tt // SUB_T)]


def _token_block(B, T, rows, per_seq_m):
    tt = min(T, rows)
    bb = 1 if per_seq_m else min(B, rows // tt)
    return bb, tt


def _mod_kernel(cond_ref, w_ref, b_ref, o_ref):
    c = cond_ref[...]
    s = (c * _sigmoid(c)).astype(BF16)
    o_ref[...] = _dot(s, w_ref[...].astype(BF16)) + b_ref[...]


def _modulation(cond, w_mod, b_mod):
    rows = cond.shape[0]
    n = w_mod.shape[1]
    bn = D
    return pl.pallas_call(
        _mod_kernel,
        grid=(n // bn,),
        in_specs=[
            pl.BlockSpec((rows, D), lambda j: (0, 0)),
            pl.BlockSpec((D, bn), lambda j: (0, j)),
            pl.BlockSpec((1, bn), lambda j: (0, j)),
        ],
        out_specs=pl.BlockSpec((rows, bn), lambda j: (0, j)),
        out_shape=jax.ShapeDtypeStruct((rows, n), F32),
        compiler_params=pltpu.CompilerParams(dimension_semantics=("arbitrary",)),
        name="modulation",
    )(cond, w_mod, b_mod.reshape(1, n))


def _stage1_kernel(has_pe, *refs):
    if has_pe:
        x_ref, per_ref, pec_ref, refs = refs[0], refs[1], refs[2], refs[3:]
    else:
        x_ref, refs = refs[0], refs[1:]
    (m_ref, n1_ref, n2_ref, gv_ref, wg_ref, wu_ref, wd_ref, win_ref,
     x1_ref, zx_ref, za_ref) = refs
    m = m_ref[0]
    subs = _subtiles(x_ref)

    xs, hs = [], []
    for b, rs in subs:
        x = x_ref[b, rs, :]
        if has_pe:
            reps = GRID_W // SUBLANES
            r0 = rs.start // GRID_W
            row_half = jnp.concatenate(
                [jnp.tile(per_ref[r * SUBLANES:(r + 1) * SUBLANES, :], (reps, 1))
                 for r in range(r0, r0 + SUB_T // GRID_W)], axis=0)
            col_half = jnp.tile(pec_ref[...], (SUB_T // GRID_W, 1))
            x = x + jnp.concatenate([row_half, col_half], axis=1)
        xs.append(x)
        hs.append(_modulate(x, n1_ref[...], m[0:1], m[1:2]).astype(BF16))

    h2s = []
    for (b, rs), x, h in zip(subs, xs, hs):
        g = _dot(h, wg_ref[...])
        u = _dot(h, wu_ref[...])
        act = ((g * _sigmoid(g)) * u).astype(BF16)
        x1 = x + (0.5 * m[2:3]) * _dot(act, wd_ref[...])
        x1_ref[b, rs, :] = x1
        h2s.append(_modulate(x1, n2_ref[...], m[3:4], m[4:5]).astype(BF16))

    branches = ((2, jax.nn.gelu), (3, lambda v: _rms(jax.nn.gelu(v), gv_ref[...])),
                (4, _sigmoid), (5, _sigmoid), (1, jax.nn.gelu))
    for (b, rs), h2 in zip(subs, h2s):
        for j, (src, fn) in enumerate(branches):
            z = _dot(h2, win_ref[:, src * D:(src + 1) * D])
            za_ref[b, rs, j * D:(j + 1) * D] = fn(z).astype(BF16)
        zx_ref[b, rs, :] = _dot(h2, win_ref[:, 0:D])


def _stage1(x, pe_tables, m, n1, n2, gv, wg, wu, wd, win):
    B, T, _ = x.shape
    has_pe = pe_tables is not None
    per_batch_m = m.shape[0] > 1
    bb, tt = _token_block(B, T, STAGE1_T, per_batch_m)
    m_map = (lambda t, b: (b, 0, 0)) if per_batch_m else (lambda t, b: (0, 0, 0))
    in_specs = [pl.BlockSpec((bb, tt, D), lambda t, b: (b, t, 0))]
    args = [x]
    if has_pe:
        pe_rows, pe_cols = pe_tables
        in_specs += [pl.BlockSpec((tt // GRID_W * SUBLANES, D // 2), lambda t, b: (t, 0)), _whole()]
        args += [pe_rows, pe_cols]
    in_specs += [pl.BlockSpec((1, N_MOD, D), m_map)] + [_whole()] * 7
    args += [m, n1, n2, gv, wg, wu, wd, win]
    blk = lambda w: pl.BlockSpec((bb, tt, w), lambda t, b: (b, t, 0))
    return pl.pallas_call(
        functools.partial(_stage1_kernel, has_pe),
        grid=(T // tt, B // bb),
        in_specs=in_specs,
        out_specs=[blk(D), blk(D), blk(Z_ACT)],
        out_shape=[jax.ShapeDtypeStruct((B, T, D), F32),
                   jax.ShapeDtypeStruct((B, T, D), F32),
                   jax.ShapeDtypeStruct((B, T, Z_ACT), BF16)],
        compiler_params=pltpu.CompilerParams(
            dimension_semantics=("arbitrary", "arbitrary"),
            vmem_limit_bytes=VMEM_LIMIT),
        name="stage1_ffn_inproj",
    )(*args)


def _gates_to_scratch(blk, xr, wg_ref, bg_ref, lam_ref, a_scr, b_scr):
    lo = blk * GATE_BLOCK
    c4 = (0.5 * LRU_C) * _log_sigmoid(lam_ref[:, lo:lo + GATE_BLOCK])
    t = jnp.tanh(_dot(xr.astype(BF16), wg_ref[blk]) + bg_ref[blk])
    t_r = t[:, 0:GATE_BLOCK]
    t_i = t[:, GATE_BLOCK:2 * GATE_BLOCK]
    log_a = t_r * c4 + c4
    a = jnp.exp(log_a)
    q = (-0.25 * jnp.tanh(log_a)) * (a * a + 1.0)
    half_mult = jnp.where(q > 0.0, q * lax.rsqrt(q), 0.0)
    w = half_mult * xr
    bx = w * t_i + w
    for half in range(GATE_HALVES):
        c = blk * GATE_HALVES + half
        a_scr[c] = a[:, _lane_chunk(half)]
        b_scr[c] = bx[:, _lane_chunk(half)]


def _recurrence(reverse, tt, first, h0_ref, put, a_scr, b_scr, o_scr, carry_scr):
    nb = SCAN_BATCH

    @pl.when(first)
    def _():
        for c in range(N_LANE_CHUNKS):
            carry_scr[c] = h0_ref[:, _lane_chunk(c)]

    def step(s, hs):
        t = (tt - 1 - s) if reverse else s
        r0 = pl.multiple_of(t * nb, SUBLANES)
        out = []
        for c in range(N_LANE_CHUNKS):
            h = a_scr[c, pl.ds(r0, nb), :] * hs[c] + b_scr[c, pl.ds(r0, nb), :]
            o_scr[c, pl.ds(r0, nb), :] = h
            out.append(h)
        return tuple(out)

    hs = lax.fori_loop(0, tt, step,
                       tuple(carry_scr[c] for c in range(N_LANE_CHUNKS)), unroll=8)
    for c in range(N_LANE_CHUNKS):
        carry_scr[c] = hs[c]
        for s in range(nb):
            put(s, c, o_scr[c, pl.ds(s, tt, stride=nb), :])
    return hs


def _scan_bwd_kernel(zc_ref, zl_ref, zrt_ref, h0_ref, cw_ref, cb_ref, wg_ref, bg_ref, lam_ref,
                     h_ref, xr_ref, ext_scr, a_scr, b_scr, o_scr, carry_scr):
    nb, tt, _ = zc_ref.shape
    j = pl.program_id(1)
    nt = pl.num_programs(1)
    ti = nt - 1 - j
    rows = tt * nb
    lead = CONV_LEFT * nb

    for c in range(N_LANE_CHUNKS):
        cols = _lane_chunk(c)
        for k in range(CONV_LEFT):
            ext_scr[c, k * nb:(k + 1) * nb, :] = jnp.where(
                ti > 0, zl_ref[:, SUBLANES - CONV_LEFT + k, cols], 0.0)
        for k in range(CONV_RIGHT):
            ext_scr[c, lead + rows + k * nb:lead + rows + (k + 1) * nb, :] = jnp.where(
                ti < nt - 1, zrt_ref[:, k, cols], 0.0)
        for s in range(nb):
            ext_scr[c, pl.ds(lead + s, tt, stride=nb), :] = zc_ref[s, :, cols]

    for blk in range(N_GATE_BLOCKS):
        parts = []
        for half in range(GATE_HALVES):
            c = blk * GATE_HALVES + half
            cols = _lane_chunk(c)
            acc = ext_scr[c, 0:rows, :] * cw_ref[0:1, cols] + cb_ref[:, cols]
            for k in range(1, CONV_W):
                acc = acc + ext_scr[c, k * nb:k * nb + rows, :] * cw_ref[k:k + 1, cols]
            parts.append(acc)
        xr = jnp.concatenate(parts, axis=1)
        xr_ref[0, :, blk * GATE_BLOCK:(blk + 1) * GATE_BLOCK] = xr
        _gates_to_scratch(blk, xr, wg_ref, bg_ref, lam_ref, a_scr, b_scr)

    def put(s, c, h):
        h_ref[s, :, _lane_chunk(c)] = h
    _recurrence(True, tt, j == 0, h0_ref, put, a_scr, b_scr, o_scr, carry_scr)


def _scan_fwd_kernel(xr_ref, h0_ref, hb_ref, gr_ref, wg_ref, bg_ref, lam_ref,
                     yrnn_ref, hlast_ref, a_scr, b_scr, o_scr, carry_scr):
    nb, tt, _ = hb_ref.shape
    for blk in range(N_GATE_BLOCKS):
        xr = xr_ref[0, :, blk * GATE_BLOCK:(blk + 1) * GATE_BLOCK]
        _gates_to_scratch(blk, xr, wg_ref, bg_ref, lam_ref, a_scr, b_scr)

    def put(s, c, hf):
        cols = _lane_chunk(c)
        yrnn_ref[s, :, cols] = (gr_ref[s, :, cols].astype(F32) * (hf + hb_ref[s, :, cols])).astype(BF16)
    hs = _recurrence(False, tt, pl.program_id(1) == 0, h0_ref, put, a_scr, b_scr, o_scr, carry_scr)
    for c in range(N_LANE_CHUNKS):
        hlast_ref[:, _lane_chunk(c)] = hs[c]


def _scan_scratch(nb, tt):
    tile = pltpu.VMEM((N_LANE_CHUNKS, tt * nb, LANES), F32)
    return [tile, tile, tile, pltpu.VMEM((N_LANE_CHUNKS, nb, LANES), F32)]


def _scan_bwd(zx, h0, conv_w, conv_b, wgate, bgate, lam):
    B, T, _ = zx.shape
    nb = SCAN_BATCH
    tt = min(T, SCAN_T)
    nt = T // tt
    tpb = tt // SUBLANES
    nhb = T // SUBLANES
    ti = lambda j: nt - 1 - j
    return pl.pallas_call(
        _scan_bwd_kernel,
        grid=(B // nb, nt),
        in_specs=[
            pl.BlockSpec((nb, tt, D), lambda g, j: (g, ti(j), 0)),
            pl.BlockSpec((nb, SUBLANES, D),
                         lambda g, j: (g, jnp.maximum(ti(j) * tpb - 1, 0), 0)),
            pl.BlockSpec((nb, SUBLANES, D),
                         lambda g, j: (g, jnp.minimum((ti(j) + 1) * tpb, nhb - 1), 0)),
            pl.BlockSpec((nb, D), lambda g, j: (g, 0)),
            _whole(), _whole(), _whole(), _whole(), _whole(),
        ],
        out_specs=[pl.BlockSpec((nb, tt, D), lambda g, j: (g, ti(j), 0)),
                   pl.BlockSpec((1, tt * nb, D), lambda g, j: (g, ti(j), 0))],
        out_shape=[jax.ShapeDtypeStruct((B, T, D), F32),
                   jax.ShapeDtypeStruct((B // nb, T * nb, D), F32)],
        scratch_shapes=[pltpu.VMEM((N_LANE_CHUNKS, (tt + CONV_W - 1) * nb, LANES), F32)]
        + _scan_scratch(nb, tt),
        compiler_params=pltpu.CompilerParams(
            dimension_semantics=("arbitrary", "arbitrary"),
            vmem_limit_bytes=VMEM_LIMIT),
        name="scan_bwd",
    )(zx, zx, zx, h0, conv_w, conv_b, wgate, bgate, lam)


def _scan_fwd(xr_tm, h0, hb, za, wgate, bgate, lam):
    B, T, _ = hb.shape
    nb = SCAN_BATCH
    tt = min(T, SCAN_T)
    tok = pl.BlockSpec((nb, tt, D), lambda g, j: (g, j, 0))
    return pl.pallas_call(
        _scan_fwd_kernel,
        grid=(B // nb, T // tt),
        in_specs=[
            pl.BlockSpec((1, tt * nb, D), lambda g, j: (g, j, 0)),
            pl.BlockSpec((nb, D), lambda g, j: (g, 0)),
            tok,
            pl.BlockSpec((nb, tt, D), lambda g, j: (g, j, ZA_RNN_GATE)),
            _whole(), _whole(), _whole(),
        ],
        out_specs=[tok, pl.BlockSpec((nb, D), lambda g, j: (g, 0))],
        out_shape=[jax.ShapeDtypeStruct((B, T, D), BF16),
                   jax.ShapeDtypeStruct((B, D), F32)],
        scratch_shapes=_scan_scratch(nb, tt),
        compiler_params=pltpu.CompilerParams(
            dimension_semantics=("arbitrary", "arbitrary"),
            vmem_limit_bytes=VMEM_LIMIT),
        name="scan_fwd",
    )(xr_tm, h0, hb, za, wgate, bgate, lam)


def _stage3_kernel(x1_ref, yrnn_ref, za_ref, m_ref, ws_ref, bs_ref,
                   wbr_ref, wbg_ref, wo_ref, n3_ref, wg_ref, wu_ref, wd_ref, nf_ref,
                   y_ref, mixed_scr):
    m = m_ref[0]
    subs = _subtiles(x1_ref)

    x2s, hs = [], []
    for i, (b, rs) in enumerate(subs):
        zcol = lambda c: za_ref[b, rs, c * D:(c + 1) * D]
        rnn_proj = _dot(yrnn_ref[b, rs, :], wbr_ref[...])
        for n in range(SUB_T // CHUNK):
            rc = slice(rs.start + n * CHUNK, rs.start + (n + 1) * CHUNK)
            for g in range(GROUPS):
                cs = _lane_chunk(g)
                mixed_scr[i, n * CHUNK:(n + 1) * CHUNK, cs] = (
                    _dot(ws_ref[g], za_ref[b, rc, D + g * LANES:D + (g + 1) * LANES]) + bs_ref[:, cs])
        y_g = zcol(0).astype(F32) * mixed_scr[i]
        g_proj = _dot(y_g.astype(BF16), wbg_ref[...])
        merged = zcol(2).astype(F32) * rnn_proj + zcol(3).astype(F32) * g_proj
        x2 = x1_ref[b, rs, :] + m[5:6] * _dot(merged.astype(BF16), wo_ref[...])
        x2s.append(x2)
        hs.append(_modulate(x2, n3_ref[...], m[6:7], m[7:8]).astype(BF16))

    for (b, rs), x2, h in zip(subs, x2s, hs):
        g = _dot(h, wg_ref[...])
        uu = _dot(h, wu_ref[...])
        act = ((g * _sigmoid(g)) * uu).astype(BF16)
        x3 = x2 + (0.5 * m[8:9]) * _dot(act, wd_ref[...])
        y_ref[b, rs, :] = _rms(x3, nf_ref[...])


def _stage3(x1, y_rnn, za, m, ws, bs, wbr, wbg, wo, n3, wg, wu, wd, nf):
    B, T, _ = x1.shape
    per_batch_m = m.shape[0] > 1
    bb, tt = _token_block(B, T, STAGE3_T, per_batch_m)
    m_map = (lambda t, b: (b, 0, 0)) if per_batch_m else (lambda t, b: (0, 0, 0))
    blk = lambda w: pl.BlockSpec((bb, tt, w), lambda t, b: (b, t, 0))
    return pl.pallas_call(
        _stage3_kernel,
        grid=(T // tt, B // bb),
        in_specs=[blk(D), blk(D), blk(ZA_MIX),
                  pl.BlockSpec((1, N_MOD, D), m_map)] + [_whole()] * 10,
        out_specs=blk(D),
        out_shape=jax.ShapeDtypeStruct((B, T, D), F32),
        scratch_shapes=[pltpu.VMEM((bb * tt // SUB_T, SUB_T, D), F32)],
        compiler_params=pltpu.CompilerParams(
            dimension_semantics=("arbitrary", "arbitrary"),
            vmem_limit_bytes=VMEM_LIMIT),
        name="stage3_mix_ffn",
    )(x1, y_rnn, za, m, ws, bs, wbr, wbg, wo, n3, wg, wu, wd, nf)


def _grid_pos_tables(n_tokens):
    q = D // 4
    freqs = 1.0 / (10000.0 ** (jnp.arange(q, dtype=F32) / q))
    half = lambda idx: jnp.concatenate(
        [jnp.sin(idx[:, None] * freqs), jnp.cos(idx[:, None] * freqs)], axis=-1)
    rows = half(jnp.arange(n_tokens // GRID_W).astype(F32))
    cols = half(jnp.arange(GRID_W).astype(F32))
    return jnp.repeat(rows, SUBLANES, axis=0), cols


def _gate_weights(w_r, w_i, b_r, b_i):
    def bd(w):
        w4 = w.reshape(N_GATE_BLOCKS, HEADS_PER_BLOCK, HEAD_DIM, HEAD_DIM)
        eye = jnp.eye(HEADS_PER_BLOCK, dtype=w.dtype)
        return jnp.einsum('ghij,hk->ghikj', w4, eye).reshape(N_GATE_BLOCKS, GATE_BLOCK, GATE_BLOCK)
    wg = (0.5 * jnp.concatenate([bd(w_r), bd(w_i)], axis=-1)).astype(BF16)
    bg = 0.5 * jnp.concatenate([b_r.reshape(N_GATE_BLOCKS, 1, GATE_BLOCK),
                                b_i.reshape(N_GATE_BLOCKS, 1, GATE_BLOCK)], axis=-1)
    return wg, bg


def kernel(x_prompt, x_sample, state_rnn_fwd, state_rnn_bwd, c, c_ctx, w_mod, b_mod, norm1, norm2, norm3, ff1_gate, ff1_up, ff1_down, w_in, conv_w, conv_b, w_r, b_r, w_i, b_i, lam, gmlp_norm, w_s, b_s, w_br, w_bg, w_out, ff2_gate, ff2_up, ff2_down, norm_f):
    l = 0
    nbatch_lat = c.shape[0]
    cond_rows = 2 * SUBLANES
    cond = jnp.zeros((cond_rows, D), F32).at[:nbatch_lat].set(c).at[nbatch_lat].set(c_ctx)
    m_all = _modulation(cond, w_mod[l], b_mod[l])
    m_lat = m_all[:nbatch_lat].reshape(nbatch_lat, N_MOD, D)
    m_ctx = m_all[nbatch_lat:nbatch_lat + 1].reshape(1, N_MOD, D)

    row = lambda v: v.reshape(1, D)
    bf = lambda w: w.astype(BF16)
    n1, n2, n3, nf, gv = row(norm1[l]), row(norm2[l]), row(norm3[l]), row(norm_f), row(gmlp_norm[l])
    wg1, wu1, wd1 = bf(ff1_gate[l]), bf(ff1_up[l]), bf(ff1_down[l])
    wg2, wu2, wd2 = bf(ff2_gate[l]), bf(ff2_up[l]), bf(ff2_down[l])
    win, wbr, wbg, wo = bf(w_in[l]), bf(w_br[l]), bf(w_bg[l]), bf(w_out[l])
    ws = bf(w_s[l])
    bs = jnp.repeat(b_s[l].T, D // GROUPS, axis=1)
    cw, cb, lam_l = conv_w[l], row(conv_b[l]), lam[l]
    (wgt_f, bgt_f), (wgt_b, bgt_b) = [
        _gate_weights(w_r[l, d], w_i[l, d], b_r[l, d], b_i[l, d]) for d in range(2)]

    def trunk(x, pe_tables, m, h0_f, h0_b):
        x1, zx, za = _stage1(x, pe_tables, m, n1, n2, gv, wg1, wu1, wd1, win)
        hb, xr_tm = _scan_bwd(zx, h0_b, cw, cb, wgt_b, bgt_b, row(lam_l[1]))
        y_rnn, hf_last = _scan_fwd(xr_tm, h0_f, hb, za, wgt_f, bgt_f, row(lam_l[0]))
        y = _stage3(x1, y_rnn, za, m, ws, bs, wbr, wbg, wo, n3, wg2, wu2, wd2, nf)
        return y, hf_last, hb

    zeros = jnp.zeros((x_prompt.shape[0], D), F32)
    y_prompt, hf_last, hb_c = trunk(x_prompt, None, m_ctx, zeros, zeros)
    new_f = hf_last[:, None, :]
    new_b = hb_c[:, 0:1, :]

    y_sample, _, _ = trunk(x_sample, _grid_pos_tables(x_sample.shape[1]), m_lat,
                           state_rnn_fwd[:, l], state_rnn_bwd[:, l])
    return (y_prompt, y_sample, new_f, new_b)
```

```python
import functools

import jax
import jax.numpy as jnp
from jax import lax
from jax.experimental import pallas as pl
from jax.experimental.pallas import tpu as pltpu

D = 1024
D_FF = 2816
N_MOD = 9
EPS = 1e-6
CHUNK = 128
GROUPS = 8
HEADS = 16
HEAD_DIM = 64
CONV_W = 4
CONV_LEFT = 2
CONV_RIGHT = CONV_W - 1 - CONV_LEFT
LRU_C = 8.0
GRID_W = 64
N_BRANCH = 5
Z_ACT = N_BRANCH * D
ZA_MIX = 4 * D
ZA_RNN_GATE = 4

LANES = 128
SUBLANES = 8
N_LANE_CHUNKS = D // LANES
HEADS_PER_BLOCK = 4
GATE_BLOCK = HEADS_PER_BLOCK * HEAD_DIM
N_GATE_BLOCKS = D // GATE_BLOCK
GATE_HALVES = GATE_BLOCK // LANES

SUB_T = 256
STAGE1_T = 512
STAGE3_T = 512
SCAN_BATCH = SUBLANES
SCAN_T = 128
VMEM_LIMIT = 61 * 1024 * 1024

F32 = jnp.float32
BF16 = jnp.bfloat16


def _rms(x, g):
    ms = jnp.mean(x * x, axis=-1, keepdims=True)
    return (x * lax.rsqrt(ms + EPS)) * g


def _modulate(x, g, shift, scale):
    return _rms(x, g) * (1.0 + scale) + shift


def _sigmoid(x):
    return 0.5 * jnp.tanh(0.5 * x) + 0.5


GELU_C1 = 0.7978845608028654
GELU_C2 = GELU_C1 * 0.044715


def _gelu(x):
    hx = 0.5 * x
    return hx * jnp.tanh(x * (GELU_C1 + GELU_C2 * (x * x))) + hx


def _log_sigmoid(x):
    return jnp.minimum(x, 0.0) - jnp.log1p(jnp.exp(-jnp.abs(x)))


def _dot(a, b):
    return jnp.dot(a, b, preferred_element_type=F32)


def _whole(memory_space=pltpu.VMEM):
    return pl.BlockSpec(memory_space=memory_space)


def _lane_chunk(c):
    return slice(c * LANES, (c + 1) * LANES)


def _subtiles(ref):
    bb, tt = ref.shape[0], ref.shape[1]
    return [(b, slice(s * SUB_T, (s + 1) * SUB_T)) for b in range(bb) for s in range(tt // SUB_T)]


def _token_block(B, T, rows, per_seq_m):
    tt = min(T, rows)
    bb = 1 if per_seq_m else min(B, rows // tt)
    return bb, tt


def _mod_kernel(cond_ref, w_ref, b_ref, o_ref):
    c = cond_ref[...]
    s = (c * _sigmoid(c)).astype(BF16)
    o_ref[...] = _dot(s, w_ref[...].astype(BF16)) + b_ref[...]


def _modulation(cond, w_mod, b_mod):
    rows = cond.shape[0]
    n = w_mod.shape[1]
    bn = D
    return pl.pallas_call(
        _mod_kernel,
        grid=(n // bn,),
        in_specs=[
            pl.BlockSpec((rows, D), lambda j: (0, 0)),
            pl.BlockSpec((D, bn), lambda j: (0, j)),
            pl.BlockSpec((1, bn), lambda j: (0, j)),
        ],
        out_specs=pl.BlockSpec((rows, bn), lambda j: (0, j)),
        out_shape=jax.ShapeDtypeStruct((rows, n), F32),
        compiler_params=pltpu.CompilerParams(dimension_semantics=("arbitrary",)),
        name="modulation",
    )(cond, w_mod, b_mod.reshape(1, n))


def _stage1_kernel(has_pe, *refs):
    if has_pe:
        x_ref, per_ref, pec_ref, refs = refs[0], refs[1], refs[2], refs[3:]
    else:
        x_ref, refs = refs[0], refs[1:]
    (m_ref, n1_ref, n2_ref, gv_ref, wg_ref, wu_ref, wd_ref, win_ref,
     x1_ref, zx_ref, za_ref) = refs
    m = m_ref[0]
    subs = _subtiles(x_ref)

    xs, hs = [], []
    for b, rs in subs:
        x = x_ref[b, rs, :]
        if has_pe:
            reps = GRID_W // SUBLANES
            r0 = rs.start // GRID_W
            row_half = jnp.concatenate(
                [jnp.tile(per_ref[r * SUBLANES:(r + 1) * SUBLANES, :], (reps, 1))
                 for r in range(r0, r0 + SUB_T // GRID_W)], axis=0)
            col_half = jnp.tile(pec_ref[...], (SUB_T // GRID_W, 1))
            x = x + jnp.concatenate([row_half, col_half], axis=1)
        xs.append(x)
        hs.append(_modulate(x, n1_ref[...], m[0:1], m[1:2]).astype(BF16))

    h2s = []
    for (b, rs), x, h in zip(subs, xs, hs):
        g = _dot(h, wg_ref[...])
        u = _dot(h, wu_ref[...])
        act = ((g * _sigmoid(g)) * u).astype(BF16)
        x1 = x + (0.5 * m[2:3]) * _dot(act, wd_ref[:, 0:D])
        x1_ref[b, rs, :] = x1
        h2s.append(_modulate(x1, n2_ref[...], m[3:4], m[4:5]).astype(BF16))

    branches = ((2, _gelu), (3, lambda v: _rms(_gelu(v), gv_ref[...])),
                (4, _sigmoid), (5, _sigmoid), (1, _gelu))
    for (b, rs), h2 in zip(subs, h2s):
        for j, (src, fn) in enumerate(branches):
            z = _dot(h2, win_ref[:, src * D:(src + 1) * D])
            za_ref[b, rs, j * D:(j + 1) * D] = fn(z).astype(BF16)
        zx_ref[b, rs, :] = _dot(h2, win_ref[:, 0:D])


def _stage1(x, pe_tables, m, n1, n2, gv, wg, wu, wd, win):
    B, T, _ = x.shape
    has_pe = pe_tables is not None
    per_batch_m = m.shape[0] > 1
    bb, tt = _token_block(B, T, STAGE1_T, per_batch_m)
    m_map = (lambda t, b: (b, 0, 0)) if per_batch_m else (lambda t, b: (0, 0, 0))
    in_specs = [pl.BlockSpec((bb, tt, D), lambda t, b: (b, t, 0))]
    args = [x]
    if has_pe:
        pe_rows, pe_cols = pe_tables
        in_specs += [pl.BlockSpec((tt // GRID_W * SUBLANES, D // 2), lambda t, b: (t, 0)), _whole()]
        args += [pe_rows, pe_cols]
    in_specs += [pl.BlockSpec((1, N_MOD, D), m_map)] + [_whole()] * 7
    args += [m, n1, n2, gv, wg, wu, wd, win]
    blk = lambda w: pl.BlockSpec((bb, tt, w), lambda t, b: (b, t, 0))
    return pl.pallas_call(
        functools.partial(_stage1_kernel, has_pe),
        grid=(T // tt, B // bb),
        in_specs=in_specs,
        out_specs=[blk(D), blk(D), blk(Z_ACT)],
        out_shape=[jax.ShapeDtypeStruct((B, T, D), F32),
                   jax.ShapeDtypeStruct((B, T, D), F32),
                   jax.ShapeDtypeStruct((B, T, Z_ACT), BF16)],
        compiler_params=pltpu.CompilerParams(
            dimension_semantics=("arbitrary", "arbitrary"),
            vmem_limit_bytes=VMEM_LIMIT),
        name="stage1_ffn_inproj",
    )(*args)


def _gates_to_scratch(blk, xr, wg_ref, bg_ref, lam_ref, a_scr, b_scr):
    lo = blk * GATE_BLOCK
    c4 = (0.5 * LRU_C) * _log_sigmoid(lam_ref[:, lo:lo + GATE_BLOCK])
    t = jnp.tanh(_dot(xr.astype(BF16), wg_ref[blk]) + bg_ref[blk])
    t_r = t[:, 0:GATE_BLOCK]
    t_i = t[:, GATE_BLOCK:2 * GATE_BLOCK]
    log_a = t_r * c4 + c4
    a = jnp.exp(log_a)
    q = (-0.25 * jnp.tanh(log_a)) * (a * a + 1.0)
    half_mult = jnp.where(q > 0.0, q * lax.rsqrt(q), 0.0)
    w = half_mult * xr
    bx = w * t_i + w
    for half in range(GATE_HALVES):
        c = blk * GATE_HALVES + half
        a_scr[c] = a[:, _lane_chunk(half)]
        b_scr[c] = bx[:, _lane_chunk(half)]


def _recurrence(reverse, tt, first, h0_ref, put, a_scr, b_scr, o_scr, carry_scr):
    nb = SCAN_BATCH

    @pl.when(first)
    def _():
        for c in range(N_LANE_CHUNKS):
            carry_scr[c] = h0_ref[:, _lane_chunk(c)]

    def step(s, hs):
        t = (tt - 1 - s) if reverse else s
        r0 = pl.multiple_of(t * nb, SUBLANES)
        out = []
        for c in range(N_LANE_CHUNKS):
            h = a_scr[c, pl.ds(r0, nb), :] * hs[c] + b_scr[c, pl.ds(r0, nb), :]
            o_scr[c, pl.ds(r0, nb), :] = h
            out.append(h)
        return tuple(out)

    hs = lax.fori_loop(0, tt, step,
                       tuple(carry_scr[c] for c in range(N_LANE_CHUNKS)), unroll=8)
    for c in range(N_LANE_CHUNKS):
        carry_scr[c] = hs[c]
        for s in range(nb):
            put(s, c, o_scr[c, pl.ds(s, tt, stride=nb), :])
    return hs


def _scan_bwd_kernel(zc_ref, zl_ref, zrt_ref, h0_ref, cw_ref, cb_ref, wg_ref, bg_ref, lam_ref,
                     h_ref, xr_ref, ext_scr, a_scr, b_scr, o_scr, carry_scr):
    nb, tt, _ = zc_ref.shape
    j = pl.program_id(1)
    nt = pl.num_programs(1)
    ti = nt - 1 - j
    rows = tt * nb
    lead = CONV_LEFT * nb

    for c in range(N_LANE_CHUNKS):
        cols = _lane_chunk(c)
        for k in range(CONV_LEFT):
            ext_scr[c, k * nb:(k + 1) * nb, :] = jnp.where(
                ti > 0, zl_ref[:, SUBLANES - CONV_LEFT + k, cols], 0.0)
        for k in range(CONV_RIGHT):
            ext_scr[c, lead + rows + k * nb:lead + rows + (k + 1) * nb, :] = jnp.where(
                ti < nt - 1, zrt_ref[:, k, cols], 0.0)
        for s in range(nb):
            ext_scr[c, pl.ds(lead + s, tt, stride=nb), :] = zc_ref[s, :, cols]

    for blk in range(N_GATE_BLOCKS):
        parts = []
        for half in range(GATE_HALVES):
            c = blk * GATE_HALVES + half
            cols = _lane_chunk(c)
            acc = ext_scr[c, 0:rows, :] * cw_ref[0:1, cols] + cb_ref[:, cols]
            for k in range(1, CONV_W):
                acc = acc + ext_scr[c, k * nb:k * nb + rows, :] * cw_ref[k:k + 1, cols]
            parts.append(acc)
        xr = jnp.concatenate(parts, axis=1)
        xr_ref[0, :, blk * GATE_BLOCK:(blk + 1) * GATE_BLOCK] = xr
        _gates_to_scratch(blk, xr, wg_ref, bg_ref, lam_ref, a_scr, b_scr)

    def put(s, c, h):
        h_ref[s, :, _lane_chunk(c)] = h
    _recurrence(True, tt, j == 0, h0_ref, put, a_scr, b_scr, o_scr, carry_scr)


def _scan_fwd_kernel(xr_ref, h0_ref, hb_ref, gr_ref, wg_ref, bg_ref, lam_ref,
                     yrnn_ref, hlast_ref, a_scr, b_scr, o_scr, carry_scr):
    nb, tt, _ = hb_ref.shape
    for blk in range(N_GATE_BLOCKS):
        xr = xr_ref[0, :, blk * GATE_BLOCK:(blk + 1) * GATE_BLOCK]
        _gates_to_scratch(blk, xr, wg_ref, bg_ref, lam_ref, a_scr, b_scr)

    def put(s, c, hf):
        cols = _lane_chunk(c)
        yrnn_ref[s, :, cols] = (gr_ref[s, :, cols].astype(F32) * (hf + hb_ref[s, :, cols])).astype(BF16)
    hs = _recurrence(False, tt, pl.program_id(1) == 0, h0_ref, put, a_scr, b_scr, o_scr, carry_scr)
    for c in range(N_LANE_CHUNKS):
        hlast_ref[:, _lane_chunk(c)] = hs[c]


def _scan_scratch(nb, tt):
    tile = pltpu.VMEM((N_LANE_CHUNKS, tt * nb, LANES), F32)
    return [tile, tile, tile, pltpu.VMEM((N_LANE_CHUNKS, nb, LANES), F32)]


def _scan_bwd(zx, h0, conv_w, conv_b, wgate, bgate, lam):
    B, T, _ = zx.shape
    nb = SCAN_BATCH
    tt = min(T, SCAN_T)
    nt = T // tt
    tpb = tt // SUBLANES
    nhb = T // SUBLANES
    ti = lambda j: nt - 1 - j
    return pl.pallas_call(
        _scan_bwd_kernel,
        grid=(B // nb, nt),
        in_specs=[
            pl.BlockSpec((nb, tt, D), lambda g, j: (g, ti(j), 0)),
            pl.BlockSpec((nb, SUBLANES, D),
                         lambda g, j: (g, jnp.maximum(ti(j) * tpb - 1, 0), 0)),
            pl.BlockSpec((nb, SUBLANES, D),
                         lambda g, j: (g, jnp.minimum((ti(j) + 1) * tpb, nhb - 1), 0)),
            pl.BlockSpec((nb, D), lambda g, j: (g, 0)),
            _whole(), _whole(), _whole(), _whole(), _whole(),
        ],
        out_specs=[pl.BlockSpec((nb, tt, D), lambda g, j: (g, ti(j), 0)),
                   pl.BlockSpec((1, tt * nb, D), lambda g, j: (g, ti(j), 0))],
        out_shape=[jax.ShapeDtypeStruct((B, T, D), F32),
                   jax.ShapeDtypeStruct((B // nb, T * nb, D), F32)],
        scratch_shapes=[pltpu.VMEM((N_LANE_CHUNKS, (tt + CONV_W - 1) * nb, LANES), F32)]
        + _scan_scratch(nb, tt),
        compiler_params=pltpu.CompilerParams(
            dimension_semantics=("arbitrary", "arbitrary"),
            vmem_limit_bytes=VMEM_LIMIT),
        name="scan_bwd",
    )(zx, zx, zx, h0, conv_w, conv_b, wgate, bgate, lam)


def _scan_fwd(xr_tm, h0, hb, za, wgate, bgate, lam):
    B, T, _ = hb.shape
    nb = SCAN_BATCH
    tt = min(T, SCAN_T)
    tok = pl.BlockSpec((nb, tt, D), lambda g, j: (g, j, 0))
    return pl.pallas_call(
        _scan_fwd_kernel,
        grid=(B // nb, T // tt),
        in_specs=[
            pl.BlockSpec((1, tt * nb, D), lambda g, j: (g, j, 0)),
            pl.BlockSpec((nb, D), lambda g, j: (g, 0)),
            tok,
            pl.BlockSpec((nb, tt, D), lambda g, j: (g, j, ZA_RNN_GATE)),
            _whole(), _whole(), _whole(),
        ],
        out_specs=[tok, pl.BlockSpec((nb, D), lambda g, j: (g, 0))],
        out_shape=[jax.ShapeDtypeStruct((B, T, D), BF16),
                   jax.ShapeDtypeStruct((B, D), F32)],
        scratch_shapes=_scan_scratch(nb, tt),
        compiler_params=pltpu.CompilerParams(
            dimension_semantics=("arbitrary", "arbitrary"),
            vmem_limit_bytes=VMEM_LIMIT),
        name="scan_fwd",
    )(xr_tm, h0, hb, za, wgate, bgate, lam)


def _stage3_kernel(x1_ref, yrnn_ref, za_ref, m_ref, ws_ref, bs_ref,
                   wbr_ref, wbg_ref, wo_ref, n3_ref, wg_ref, wu_ref, wd_ref, nf_ref,
                   y_ref, mixed_scr):
    m = m_ref[0]
    subs = _subtiles(x1_ref)

    x2s, hs = [], []
    for i, (b, rs) in enumerate(subs):
        zcol = lambda c: za_ref[b, rs, c * D:(c + 1) * D]
        rnn_proj = _dot(yrnn_ref[b, rs, :], wbr_ref[:, 0:D])
        for n in range(SUB_T // CHUNK):
            rc = slice(rs.start + n * CHUNK, rs.start + (n + 1) * CHUNK)
            for g in range(GROUPS):
                cs = _lane_chunk(g)
                mixed_scr[i, n * CHUNK:(n + 1) * CHUNK, cs] = (
                    _dot(ws_ref[g], za_ref[b, rc, D + g * LANES:D + (g + 1) * LANES]) + bs_ref[:, cs])
        y_g = zcol(0).astype(F32) * mixed_scr[i]
        g_proj = _dot(y_g.astype(BF16), wbg_ref[:, 0:D])
        merged = zcol(2).astype(F32) * rnn_proj + zcol(3).astype(F32) * g_proj
        x2 = x1_ref[b, rs, :] + m[5:6] * _dot(merged.astype(BF16), wo_ref[:, 0:D])
        x2s.append(x2)
        hs.append(_modulate(x2, n3_ref[...], m[6:7], m[7:8]).astype(BF16))

    for (b, rs), x2, h in zip(subs, x2s, hs):
        g = _dot(h, wg_ref[...])
        uu = _dot(h, wu_ref[...])
        act = ((g * _sigmoid(g)) * uu).astype(BF16)
        x3 = x2 + (0.5 * m[8:9]) * _dot(act, wd_ref[:, 0:D])
        y_ref[b, rs, :] = _rms(x3, nf_ref[...])


def _stage3(x1, y_rnn, za, m, ws, bs, wbr, wbg, wo, n3, wg, wu, wd, nf):
    B, T, _ = x1.shape
    per_batch_m = m.shape[0] > 1
    bb, tt = _token_block(B, T, STAGE3_T, per_batch_m)
    m_map = (lambda t, b: (b, 0, 0)) if per_batch_m else (lambda t, b: (0, 0, 0))
    blk = lambda w: pl.BlockSpec((bb, tt, w), lambda t, b: (b, t, 0))
    return pl.pallas_call(
        _stage3_kernel,
        grid=(T // tt, B // bb),
        in_specs=[blk(D), blk(D), blk(ZA_MIX),
                  pl.BlockSpec((1, N_MOD, D), m_map)] + [_whole()] * 10,
        out_specs=blk(D),
        out_shape=jax.ShapeDtypeStruct((B, T, D), F32),
        scratch_shapes=[pltpu.VMEM((bb * tt // SUB_T, SUB_T, D), F32)],
        compiler_params=pltpu.CompilerParams(
            dimension_semantics=("arbitrary", "arbitrary"),
            vmem_limit_bytes=VMEM_LIMIT),
        name="stage3_mix_ffn",
    )(x1, y_rnn, za, m, ws, bs, wbr, wbg, wo, n3, wg, wu, wd, nf)


def _grid_pos_tables(n_tokens):
    q = D // 4
    freqs = 1.0 / (10000.0 ** (jnp.arange(q, dtype=F32) / q))
    half = lambda idx: jnp.concatenate(
        [jnp.sin(idx[:, None] * freqs), jnp.cos(idx[:, None] * freqs)], axis=-1)
    rows = half(jnp.arange(n_tokens // GRID_W).astype(F32))
    cols = half(jnp.arange(GRID_W).astype(F32))
    return jnp.repeat(rows, SUBLANES, axis=0), cols


def _gate_weights(w_r, w_i, b_r, b_i):
    def bd(w):
        w4 = w.reshape(N_GATE_BLOCKS, HEADS_PER_BLOCK, HEAD_DIM, HEAD_DIM)
        eye = jnp.eye(HEADS_PER_BLOCK, dtype=w.dtype)
        return jnp.einsum('ghij,hk->ghikj', w4, eye).reshape(N_GATE_BLOCKS, GATE_BLOCK, GATE_BLOCK)
    wg = (0.5 * jnp.concatenate([bd(w_r), bd(w_i)], axis=-1)).astype(BF16)
    bg = 0.5 * jnp.concatenate([b_r.reshape(N_GATE_BLOCKS, 1, GATE_BLOCK),
                                b_i.reshape(N_GATE_BLOCKS, 1, GATE_BLOCK)], axis=-1)
    return wg, bg


def kernel(x_prompt, x_sample, state_rnn_fwd, state_rnn_bwd, c, c_ctx, w_mod, b_mod, norm1, norm2, norm3, ff1_gate, ff1_up, ff1_down, w_in, conv_w, conv_b, w_r, b_r, w_i, b_i, lam, gmlp_norm, w_s, b_s, w_br, w_bg, w_out, ff2_gate, ff2_up, ff2_down, norm_f):
    l = 0
    nbatch_lat = c.shape[0]
    cond_rows = 2 * SUBLANES
    cond = jnp.zeros((cond_rows, D), F32).at[:nbatch_lat].set(c).at[nbatch_lat].set(c_ctx)
    m_all = _modulation(cond, w_mod[l], b_mod[l])
    m_lat = m_all[:nbatch_lat].reshape(nbatch_lat, N_MOD, D)
    m_ctx = m_all[nbatch_lat:nbatch_lat + 1].reshape(1, N_MOD, D)

    row = lambda v: v.reshape(1, D)
    bf = lambda w: w.astype(BF16)

    def bfp(w):
        assert (w.shape[-1] // LANES) % 2 == 0
        return jnp.pad(w.astype(BF16), ((0, 0), (0, LANES)))
    n1, n2, n3, nf, gv = row(norm1[l]), row(norm2[l]), row(norm3[l]), row(norm_f), row(gmlp_norm[l])
    wg1, wu1, wd1 = bf(ff1_gate[l]), bf(ff1_up[l]), bfp(ff1_down[l])
    wg2, wu2, wd2 = bf(ff2_gate[l]), bf(ff2_up[l]), bfp(ff2_down[l])
    win, wbr, wbg, wo = bfp(w_in[l]), bfp(w_br[l]), bfp(w_bg[l]), bfp(w_out[l])
    ws = bf(w_s[l])
    bs = jnp.repeat(b_s[l].T, D // GROUPS, axis=1)
    cw, cb, lam_l = conv_w[l], row(conv_b[l]), lam[l]
    (wgt_f, bgt_f), (wgt_b, bgt_b) = [
        _gate_weights(w_r[l, d], w_i[l, d], b_r[l, d], b_i[l, d]) for d in range(2)]

    def trunk(x, pe_tables, m, h0_f, h0_b):
        x1, zx, za = _stage1(x, pe_tables, m, n1, n2, gv, wg1, wu1, wd1, win)
        hb, xr_tm = _scan_bwd(zx, h0_b, cw, cb, wgt_b, bgt_b, row(lam_l[1]))
        y_rnn, hf_last = _scan_fwd(xr_tm, h0_f, hb, za, wgt_f, bgt_f, row(lam_l[0]))
        y = _stage3(x1, y_rnn, za, m, ws, bs, wbr, wbg, wo, n3, wg2, wu2, wd2, nf)
        return y, hf_last, hb

    zeros = jnp.zeros((x_prompt.shape[0], D), F32)
    y_prompt, hf_last, hb_c = trunk(x_prompt, None, m_ctx, zeros, zeros)
    new_f = hf_last[:, None, :]
    new_b = hb_c[:, 0:1, :]

    y_sample, _, _ = trunk(x_sample, _grid_pos_tables(x_sample.shape[1]), m_lat,
                           state_rnn_fwd[:, l], state_rnn_bwd[:, l])
    return (y_prompt, y_sample, new_f, new_b)
```

```python
import functools

import jax
import jax.numpy as jnp
from jax import lax
from jax.experimental import pallas as pl
from jax.experimental.pallas import tpu as pltpu

D = 1024
D_FF = 2816
N_MOD = 9
EPS = 1e-6
CHUNK = 128
GROUPS = 8
HEADS = 16
HEAD_DIM = 64
CONV_W = 4
CONV_LEFT = 2
CONV_RIGHT = CONV_W - 1 - CONV_LEFT
LRU_C = 8.0
GRID_W = 64
N_BRANCH = 5
Z_ACT = N_BRANCH * D
ZA_MIX = 4 * D
ZA_RNN_GATE = 4

LANES = 128
SUBLANES = 8
N_LANE_CHUNKS = D // LANES
HEADS_PER_BLOCK = 4
GATE_BLOCK = HEADS_PER_BLOCK * HEAD_DIM
N_GATE_BLOCKS = D // GATE_BLOCK
GATE_HALVES = GATE_BLOCK // LANES

SUB_T = 256
STAGE1_T = 512
STAGE3_T = 512
SCAN_BATCH = SUBLANES
SCAN_T = 128
VMEM_LIMIT = 61 * 1024 * 1024

F32 = jnp.float32
BF16 = jnp.bfloat16


def _rms(x, g):
    ms = jnp.mean(x * x, axis=-1, keepdims=True)
    return (x * lax.rsqrt(ms + EPS)) * g


def _modulate(x, g, shift, scale):
    return _rms(x, g * (1.0 + scale)) + shift


def _sigmoid(x):
    return 0.5 * jnp.tanh(0.5 * x) + 0.5


def _sigmoid_of_half(hx):
    return 0.5 * jnp.tanh(hx) + 0.5


def _silu_of_half(hx):
    return hx * jnp.tanh(hx) + hx


GELU_K1 = 2.0 * 0.7978845608028654
GELU_K2 = 4.0 * GELU_K1 * 0.044715


def _gelu_of_half(hx):
    return hx * jnp.tanh(hx * (GELU_K1 + GELU_K2 * (hx * hx))) + hx


def _log_sigmoid(x):
    return jnp.minimum(x, 0.0) - jnp.log1p(jnp.exp(-jnp.abs(x)))


def _dot(a, b):
    return jnp.dot(a, b, preferred_element_type=F32)


def _whole(memory_space=pltpu.VMEM):
    return pl.BlockSpec(memory_space=memory_space)


def _lane_chunk(c):
    return slice(c * LANES, (c + 1) * LANES)


def _subtiles(ref):
    bb, tt = ref.shape[0], ref.shape[1]
    return [(b, slice(s * SUB_T, (s + 1) * SUB_T)) for b in range(bb) for s in range(tt // SUB_T)]


def _token_block(B, T, rows, per_seq_m):
    tt = min(T, rows)
    bb = 1 if per_seq_m else min(B, rows // tt)
    return bb, tt


def _mod_kernel(cond_ref, w_ref, b_ref, o_ref):
    c = cond_ref[...]
    s = (c * _sigmoid(c)).astype(BF16)
    o_ref[...] = _dot(s, w_ref[...].astype(BF16)) + b_ref[...]


def _modulation(cond, w_mod, b_mod):
    rows = cond.shape[0]
    n = w_mod.shape[1]
    bn = D
    return pl.pallas_call(
        _mod_kernel,
        grid=(n // bn,),
        in_specs=[
            pl.BlockSpec((rows, D), lambda j: (0, 0)),
            pl.BlockSpec((D, bn), lambda j: (0, j)),
            pl.BlockSpec((1, bn), lambda j: (0, j)),
        ],
        out_specs=pl.BlockSpec((rows, bn), lambda j: (0, j)),
        out_shape=jax.ShapeDtypeStruct((rows, n), F32),
        compiler_params=pltpu.CompilerParams(dimension_semantics=("arbitrary",)),
        name="modulation",
    )(cond, w_mod, b_mod.reshape(1, n))


def _stage1_kernel(has_pe, *refs):
    if has_pe:
        x_ref, per_ref, pec_ref, refs = refs[0], refs[1], refs[2], refs[3:]
    else:
        x_ref, refs = refs[0], refs[1:]
    (m_ref, n1_ref, n2_ref, gv_ref, wg_ref, wu_ref, wd_ref, win_ref,
     x1_ref, zx_ref, za_ref) = refs
    m = m_ref[0]
    subs = _subtiles(x_ref)

    xs, hs = [], []
    for b, rs in subs:
        x = x_ref[b, rs, :]
        if has_pe:
            reps = GRID_W // SUBLANES
            r0 = rs.start // GRID_W
            row_half = jnp.concatenate(
                [jnp.tile(per_ref[r * SUBLANES:(r + 1) * SUBLANES, :], (reps, 1))
                 for r in range(r0, r0 + SUB_T // GRID_W)], axis=0)
            col_half = jnp.tile(pec_ref[...], (SUB_T // GRID_W, 1))
            x = x + jnp.concatenate([row_half, col_half], axis=1)
        xs.append(x)
        hs.append(_modulate(x, n1_ref[...], m[0:1], m[1:2]).astype(BF16))

    h2s = []
    for (b, rs), x, h in zip(subs, xs, hs):
        g = _dot(h, wg_ref[...])
        u = _dot(h, wu_ref[...])
        act = (_silu_of_half(g) * u).astype(BF16)
        x1 = x + (0.5 * m[2:3]) * _dot(act, wd_ref[:, 0:D])
        x1_ref[b, rs, :] = x1
        h2s.append(_modulate(x1, n2_ref[...], m[3:4], m[4:5]).astype(BF16))

    branches = ((2, _gelu_of_half), (3, lambda v: _rms(_gelu_of_half(v), gv_ref[...])),
                (4, _sigmoid_of_half), (5, _sigmoid_of_half), (1, _gelu_of_half))
    for (b, rs), h2 in zip(subs, h2s):
        for j, (src, fn) in enumerate(branches):
            z = _dot(h2, win_ref[:, src * D:(src + 1) * D])
            za_ref[b, rs, j * D:(j + 1) * D] = fn(z).astype(BF16)
        zx_ref[b, rs, :] = _dot(h2, win_ref[:, 0:D])


def _stage1(x, pe_tables, m, n1, n2, gv, wg, wu, wd, win):
    B, T, _ = x.shape
    has_pe = pe_tables is not None
    per_batch_m = m.shape[0] > 1
    bb, tt = _token_block(B, T, STAGE1_T, per_batch_m)
    m_map = (lambda t, b: (b, 0, 0)) if per_batch_m else (lambda t, b: (0, 0, 0))
    in_specs = [pl.BlockSpec((bb, tt, D), lambda t, b: (b, t, 0))]
    args = [x]
    if has_pe:
        pe_rows, pe_cols = pe_tables
        in_specs += [pl.BlockSpec((tt // GRID_W * SUBLANES, D // 2), lambda t, b: (t, 0)), _whole()]
        args += [pe_rows, pe_cols]
    in_specs += [pl.BlockSpec((1, N_MOD, D), m_map)] + [_whole()] * 7
    args += [m, n1, n2, gv, wg, wu, wd, win]
    blk = lambda w: pl.BlockSpec((bb, tt, w), lambda t, b: (b, t, 0))
    return pl.pallas_call(
        functools.partial(_stage1_kernel, has_pe),
        grid=(T // tt, B // bb),
        in_specs=in_specs,
        out_specs=[blk(D), blk(D), blk(Z_ACT)],
        out_shape=[jax.ShapeDtypeStruct((B, T, D), F32),
                   jax.ShapeDtypeStruct((B, T, D), F32),
                   jax.ShapeDtypeStruct((B, T, Z_ACT), BF16)],
        compiler_params=pltpu.CompilerParams(
            dimension_semantics=("arbitrary", "arbitrary"),
            vmem_limit_bytes=VMEM_LIMIT),
        name="stage1_ffn_inproj",
    )(*args)


def _gates_to_scratch(blk, xr, wg_ref, bg_ref, lam_ref, a_scr, b_scr):
    lo = blk * GATE_BLOCK
    c4 = (0.5 * LRU_C) * _log_sigmoid(lam_ref[:, lo:lo + GATE_BLOCK])
    t = jnp.tanh(_dot(xr.astype(BF16), wg_ref[blk]) + bg_ref[blk])
    t_r = t[:, 0:GATE_BLOCK]
    t_i = t[:, GATE_BLOCK:2 * GATE_BLOCK]
    log_a = t_r * c4 + c4
    a = jnp.exp(log_a)
    q = (-0.25 * jnp.tanh(log_a)) * (a * a + 1.0)
    half_mult = jnp.where(q > 0.0, q * lax.rsqrt(q), 0.0)
    w = half_mult * xr
    bx = w * t_i + w
    for half in range(GATE_HALVES):
        c = blk * GATE_HALVES + half
        a_scr[c] = a[:, _lane_chunk(half)]
        b_scr[c] = bx[:, _lane_chunk(half)]


def _recurrence(reverse, tt, first, h0_ref, put, a_scr, b_scr, o_scr, carry_scr):
    nb = SCAN_BATCH

    @pl.when(first)
    def _():
        for c in range(N_LANE_CHUNKS):
            carry_scr[c] = h0_ref[:, _lane_chunk(c)]

    def step(s, hs):
        t = (tt - 1 - s) if reverse else s
        r0 = pl.multiple_of(t * nb, SUBLANES)
        out = []
        for c in range(N_LANE_CHUNKS):
            h = a_scr[c, pl.ds(r0, nb), :] * hs[c] + b_scr[c, pl.ds(r0, nb), :]
            o_scr[c, pl.ds(r0, nb), :] = h
            out.append(h)
        return tuple(out)

    hs = lax.fori_loop(0, tt, step,
                       tuple(carry_scr[c] for c in range(N_LANE_CHUNKS)), unroll=8)
    for c in range(N_LANE_CHUNKS):
        carry_scr[c] = hs[c]
        for s in range(nb):
            put(s, c, o_scr[c, pl.ds(s, tt, stride=nb), :])
    return hs


def _scan_bwd_kernel(zc_ref, zl_ref, zrt_ref, h0_ref, cw_ref, cb_ref, wg_ref, bg_ref, lam_ref,
                     h_ref, xr_ref, ext_scr, a_scr, b_scr, o_scr, carry_scr):
    nb, tt, _ = zc_ref.shape
    j = pl.program_id(1)
    nt = pl.num_programs(1)
    ti = nt - 1 - j
    rows = tt * nb
    lead = CONV_LEFT * nb

    for c in range(N_LANE_CHUNKS):
        cols = _lane_chunk(c)
        for k in range(CONV_LEFT):
            ext_scr[c, k * nb:(k + 1) * nb, :] = jnp.where(
                ti > 0, zl_ref[:, SUBLANES - CONV_LEFT + k, cols], 0.0)
        for k in range(CONV_RIGHT):
            ext_scr[c, lead + rows + k * nb:lead + rows + (k + 1) * nb, :] = jnp.where(
                ti < nt - 1, zrt_ref[:, k, cols], 0.0)
        for s in range(nb):
            ext_scr[c, pl.ds(lead + s, tt, stride=nb), :] = zc_ref[s, :, cols]

    for blk in range(N_GATE_BLOCKS):
        parts = []
        for half in range(GATE_HALVES):
            c = blk * GATE_HALVES + half
            cols = _lane_chunk(c)
            acc = ext_scr[c, 0:rows, :] * cw_ref[0:1, cols] + cb_ref[:, cols]
            for k in range(1, CONV_W):
                acc = acc + ext_scr[c, k * nb:k * nb + rows, :] * cw_ref[k:k + 1, cols]
            parts.append(acc)
        xr = jnp.concatenate(parts, axis=1)
        xr_ref[0, :, blk * GATE_BLOCK:(blk + 1) * GATE_BLOCK] = xr
        _gates_to_scratch(blk, xr, wg_ref, bg_ref, lam_ref, a_scr, b_scr)

    def put(s, c, h):
        h_ref[s, :, _lane_chunk(c)] = h
    _recurrence(True, tt, j == 0, h0_ref, put, a_scr, b_scr, o_scr, carry_scr)


def _scan_fwd_kernel(xr_ref, h0_ref, hb_ref, gr_ref, wg_ref, bg_ref, lam_ref,
                     yrnn_ref, hlast_ref, a_scr, b_scr, o_scr, carry_scr):
    nb, tt, _ = hb_ref.shape
    for blk in range(N_GATE_BLOCKS):
        xr = xr_ref[0, :, blk * GATE_BLOCK:(blk + 1) * GATE_BLOCK]
        _gates_to_scratch(blk, xr, wg_ref, bg_ref, lam_ref, a_scr, b_scr)

    def put(s, c, hf):
        cols = _lane_chunk(c)
        yrnn_ref[s, :, cols] = (gr_ref[s, :, cols].astype(F32) * (hf + hb_ref[s, :, cols])).astype(BF16)
    hs = _recurrence(False, tt, pl.program_id(1) == 0, h0_ref, put, a_scr, b_scr, o_scr, carry_scr)
    for c in range(N_LANE_CHUNKS):
        hlast_ref[:, _lane_chunk(c)] = hs[c]


def _scan_scratch(nb, tt):
    tile = pltpu.VMEM((N_LANE_CHUNKS, tt * nb, LANES), F32)
    return [tile, tile, tile, pltpu.VMEM((N_LANE_CHUNKS, nb, LANES), F32)]


def _scan_bwd(zx, h0, conv_w, conv_b, wgate, bgate, lam):
    B, T, _ = zx.shape
    nb = SCAN_BATCH
    tt = min(T, SCAN_T)
    nt = T // tt
    tpb = tt // SUBLANES
    nhb = T // SUBLANES
    ti = lambda j: nt - 1 - j
    return pl.pallas_call(
        _scan_bwd_kernel,
        grid=(B // nb, nt),
        in_specs=[
            pl.BlockSpec((nb, tt, D), lambda g, j: (g, ti(j), 0)),
            pl.BlockSpec((nb, SUBLANES, D),
                         lambda g, j: (g, jnp.maximum(ti(j) * tpb - 1, 0), 0)),
            pl.BlockSpec((nb, SUBLANES, D),
                         lambda g, j: (g, jnp.minimum((ti(j) + 1) * tpb, nhb - 1), 0)),
            pl.BlockSpec((nb, D), lambda g, j: (g, 0)),
            _whole(), _whole(), _whole(), _whole(), _whole(),
        ],
        out_specs=[pl.BlockSpec((nb, tt, D), lambda g, j: (g, ti(j), 0)),
                   pl.BlockSpec((1, tt * nb, D), lambda g, j: (g, ti(j), 0))],
        out_shape=[jax.ShapeDtypeStruct((B, T, D), F32),
                   jax.ShapeDtypeStruct((B // nb, T * nb, D), F32)],
        scratch_shapes=[pltpu.VMEM((N_LANE_CHUNKS, (tt + CONV_W - 1) * nb, LANES), F32)]
        + _scan_scratch(nb, tt),
        compiler_params=pltpu.CompilerParams(
            dimension_semantics=("arbitrary", "arbitrary"),
            vmem_limit_bytes=VMEM_LIMIT),
        name="scan_bwd",
    )(zx, zx, zx, h0, conv_w, conv_b, wgate, bgate, lam)


def _scan_fwd(xr_tm, h0, hb, za, wgate, bgate, lam):
    B, T, _ = hb.shape
    nb = SCAN_BATCH
    tt = min(T, SCAN_T)
    tok = pl.BlockSpec((nb, tt, D), lambda g, j: (g, j, 0))
    return pl.pallas_call(
        _scan_fwd_kernel,
        grid=(B // nb, T // tt),
        in_specs=[
            pl.BlockSpec((1, tt * nb, D), lambda g, j: (g, j, 0)),
            pl.BlockSpec((nb, D), lambda g, j: (g, 0)),
            tok,
            pl.BlockSpec((nb, tt, D), lambda g, j: (g, j, ZA_RNN_GATE)),
            _whole(), _whole(), _whole(),
        ],
        out_specs=[tok, pl.BlockSpec((nb, D), lambda g, j: (g, 0))],
        out_shape=[jax.ShapeDtypeStruct((B, T, D), BF16),
                   jax.ShapeDtypeStruct((B, D), F32)],
        scratch_shapes=_scan_scratch(nb, tt),
        compiler_params=pltpu.CompilerParams(
            dimension_semantics=("arbitrary", "arbitrary"),
            vmem_limit_bytes=VMEM_LIMIT),
        name="scan_fwd",
    )(xr_tm, h0, hb, za, wgate, bgate, lam)


def _stage3_kernel(x1_ref, yrnn_ref, za_ref, m_ref, ws_ref, bs_ref,
                   wbr_ref, wbg_ref, wo_ref, n3_ref, wg_ref, wu_ref, wd_ref, nf_ref,
                   y_ref, mixed_scr):
    m = m_ref[0]
    subs = _subtiles(x1_ref)

    x2s, hs = [], []
    for i, (b, rs) in enumerate(subs):
        zcol = lambda c: za_ref[b, rs, c * D:(c + 1) * D]
        rnn_proj = _dot(yrnn_ref[b, rs, :], wbr_ref[:, 0:D])
        for n in range(SUB_T // CHUNK):
            rc = slice(rs.start + n * CHUNK, rs.start + (n + 1) * CHUNK)
            for g in range(GROUPS):
                cs = _lane_chunk(g)
                mixed_scr[i, n * CHUNK:(n + 1) * CHUNK, cs] = (
                    _dot(ws_ref[g], za_ref[b, rc, D + g * LANES:D + (g + 1) * LANES]) + bs_ref[:, cs])
        y_g = zcol(0).astype(F32) * mixed_scr[i]
        g_proj = _dot(y_g.astype(BF16), wbg_ref[:, 0:D])
        merged = zcol(2).astype(F32) * rnn_proj + zcol(3).astype(F32) * g_proj
        x2 = x1_ref[b, rs, :] + m[5:6] * _dot(merged.astype(BF16), wo_ref[:, 0:D])
        x2s.append(x2)
        hs.append(_modulate(x2, n3_ref[...], m[6:7], m[7:8]).astype(BF16))

    for (b, rs), x2, h in zip(subs, x2s, hs):
        g = _dot(h, wg_ref[...])
        uu = _dot(h, wu_ref[...])
        act = (_silu_of_half(g) * uu).astype(BF16)
        x3 = x2 + (0.5 * m[8:9]) * _dot(act, wd_ref[:, 0:D])
        y_ref[b, rs, :] = _rms(x3, nf_ref[...])


def _stage3(x1, y_rnn, za, m, ws, bs, wbr, wbg, wo, n3, wg, wu, wd, nf):
    B, T, _ = x1.shape
    per_batch_m = m.shape[0] > 1
    bb, tt = _token_block(B, T, STAGE3_T, per_batch_m)
    m_map = (lambda t, b: (b, 0, 0)) if per_batch_m else (lambda t, b: (0, 0, 0))
    blk = lambda w: pl.BlockSpec((bb, tt, w), lambda t, b: (b, t, 0))
    return pl.pallas_call(
        _stage3_kernel,
        grid=(T // tt, B // bb),
        in_specs=[blk(D), blk(D), blk(ZA_MIX),
                  pl.BlockSpec((1, N_MOD, D), m_map)] + [_whole()] * 10,
        out_specs=blk(D),
        out_shape=jax.ShapeDtypeStruct((B, T, D), F32),
        scratch_shapes=[pltpu.VMEM((bb * tt // SUB_T, SUB_T, D), F32)],
        compiler_params=pltpu.CompilerParams(
            dimension_semantics=("arbitrary", "arbitrary"),
            vmem_limit_bytes=VMEM_LIMIT),
        name="stage3_mix_ffn",
    )(x1, y_rnn, za, m, ws, bs, wbr, wbg, wo, n3, wg, wu, wd, nf)


def _grid_pos_tables(n_tokens):
    q = D // 4
    freqs = 1.0 / (10000.0 ** (jnp.arange(q, dtype=F32) / q))
    half = lambda idx: jnp.concatenate(
        [jnp.sin(idx[:, None] * freqs), jnp.cos(idx[:, None] * freqs)], axis=-1)
    rows = half(jnp.arange(n_tokens // GRID_W).astype(F32))
    cols = half(jnp.arange(GRID_W).astype(F32))
    return jnp.repeat(rows, SUBLANES, axis=0), cols


def _gate_weights(w_r, w_i, b_r, b_i):
    def bd(w):
        w4 = w.reshape(N_GATE_BLOCKS, HEADS_PER_BLOCK, HEAD_DIM, HEAD_DIM)
        eye = jnp.eye(HEADS_PER_BLOCK, dtype=w.dtype)
        return jnp.einsum('ghij,hk->ghikj', w4, eye).reshape(N_GATE_BLOCKS, GATE_BLOCK, GATE_BLOCK)
    wg = (0.5 * jnp.concatenate([bd(w_r), bd(w_i)], axis=-1)).astype(BF16)
    bg = 0.5 * jnp.concatenate([b_r.reshape(N_GATE_BLOCKS, 1, GATE_BLOCK),
                                b_i.reshape(N_GATE_BLOCKS, 1, GATE_BLOCK)], axis=-1)
    return wg, bg


def kernel(x_prompt, x_sample, state_rnn_fwd, state_rnn_bwd, c, c_ctx, w_mod, b_mod, norm1, norm2, norm3, ff1_gate, ff1_up, ff1_down, w_in, conv_w, conv_b, w_r, b_r, w_i, b_i, lam, gmlp_norm, w_s, b_s, w_br, w_bg, w_out, ff2_gate, ff2_up, ff2_down, norm_f):
    l = 0
    nbatch_lat = c.shape[0]
    cond_rows = 2 * SUBLANES
    cond = jnp.zeros((cond_rows, D), F32).at[:nbatch_lat].set(c).at[nbatch_lat].set(c_ctx)
    m_all = _modulation(cond, w_mod[l], b_mod[l])
    m_lat = m_all[:nbatch_lat].reshape(nbatch_lat, N_MOD, D)
    m_ctx = m_all[nbatch_lat:nbatch_lat + 1].reshape(1, N_MOD, D)

    row = lambda v: v.reshape(1, D)
    bf = lambda w: w.astype(BF16)

    def bfp(w):
        assert (w.shape[-1] // LANES) % 2 == 0
        return jnp.pad(w.astype(BF16), ((0, 0), (0, LANES)))
    n1, n2, n3, nf, gv = row(norm1[l]), row(norm2[l]), row(norm3[l]), row(norm_f), row(gmlp_norm[l])
    wg1, wu1, wd1 = bf(0.5 * ff1_gate[l]), bf(ff1_up[l]), bfp(ff1_down[l])
    wg2, wu2, wd2 = bf(0.5 * ff2_gate[l]), bf(ff2_up[l]), bfp(ff2_down[l])
    in_scale = jnp.concatenate([jnp.ones((D,), F32), jnp.full((N_BRANCH * D,), 0.5, F32)])
    win, wbr, wbg, wo = bfp(w_in[l] * in_scale), bfp(w_br[l]), bfp(w_bg[l]), bfp(w_out[l])
    ws = bf(w_s[l])
    bs = jnp.repeat(b_s[l].T, D // GROUPS, axis=1)
    cw, cb, lam_l = conv_w[l], row(conv_b[l]), lam[l]
    (wgt_f, bgt_f), (wgt_b, bgt_b) = [
        _gate_weights(w_r[l, d], w_i[l, d], b_r[l, d], b_i[l, d]) for d in range(2)]

    def trunk(x, pe_tables, m, h0_f, h0_b):
        x1, zx, za = _stage1(x, pe_tables, m, n1, n2, gv, wg1, wu1, wd1, win)
        hb, xr_tm = _scan_bwd(zx, h0_b, cw, cb, wgt_b, bgt_b, row(lam_l[1]))
        y_rnn, hf_last = _scan_fwd(xr_tm, h0_f, hb, za, wgt_f, bgt_f, row(lam_l[0]))
        y = _stage3(x1, y_rnn, za, m, ws, bs, wbr, wbg, wo, n3, wg2, wu2, wd2, nf)
        return y, hf_last, hb

    zeros = jnp.zeros((x_prompt.shape[0], D), F32)
    y_prompt, hf_last, hb_c = trunk(x_prompt, None, m_ctx, zeros, zeros)
    new_f = hf_last[:, None, :]
    new_b = hb_c[:, 0:1, :]

    y_sample, _, _ = trunk(x_sample, _grid_pos_tables(x_sample.shape[1]), m_lat,
                           state_rnn_fwd[:, l], state_rnn_bwd[:, l])
    return (y_prompt, y_sample, new_f, new_b)
```

```python
import functools

import jax
import jax.numpy as jnp
from jax import lax
from jax.experimental import pallas as pl
from jax.experimental.pallas import tpu as pltpu

D = 1024
D_FF = 2816
N_MOD = 9
EPS = 1e-6
CHUNK = 128
GROUPS = 8
HEADS = 16
HEAD_DIM = 64
CONV_W = 4
CONV_LEFT = 2
CONV_RIGHT = CONV_W - 1 - CONV_LEFT
LRU_C = 8.0
GRID_W = 64
N_BRANCH = 5
Z_ACT = N_BRANCH * D
ZA_MIX = 4 * D
ZA_RNN_GATE = 4

LANES = 128
SUBLANES = 8
N_LANE_CHUNKS = D // LANES
HEADS_PER_BLOCK = 4
GATE_BLOCK = HEADS_PER_BLOCK * HEAD_DIM
N_GATE_BLOCKS = D // GATE_BLOCK
GATE_HALVES = GATE_BLOCK // LANES

SUB_T = 256
STAGE1_T = 512
STAGE3_T = 512
SCAN_BATCH = SUBLANES
SCAN_T = 128
SCAN_UNROLL = 8
VMEM_LIMIT = 61 * 1024 * 1024

F32 = jnp.float32
BF16 = jnp.bfloat16


def _rms(x, g):
    ms = jnp.mean(x * x, axis=-1, keepdims=True)
    return (x * lax.rsqrt(ms + EPS)) * g


def _modulate(x, g, shift, scale):
    return _rms(x, g * (1.0 + scale)) + shift


def _sigmoid(x):
    return 0.5 * jnp.tanh(0.5 * x) + 0.5


def _sigmoid_of_half(hx):
    return 0.5 * jnp.tanh(hx) + 0.5


def _silu_of_half(hx):
    return hx * jnp.tanh(hx) + hx


GELU_K1 = 2.0 * 0.7978845608028654
GELU_K2 = 4.0 * GELU_K1 * 0.044715


def _gelu_of_half(hx):
    return hx * jnp.tanh(hx * (GELU_K1 + GELU_K2 * (hx * hx))) + hx


def _log_sigmoid(x):
    return jnp.minimum(x, 0.0) - jnp.log1p(jnp.exp(-jnp.abs(x)))


def _dot(a, b):
    return jnp.dot(a, b, preferred_element_type=F32)


def _whole(memory_space=pltpu.VMEM):
    return pl.BlockSpec(memory_space=memory_space)


def _lane_chunk(c):
    return slice(c * LANES, (c + 1) * LANES)


def _subtiles(ref):
    bb, tt = ref.shape[0], ref.shape[1]
    return [(b, slice(s * SUB_T, (s + 1) * SUB_T)) for b in range(bb) for s in range(tt // SUB_T)]


def _token_block(B, T, rows, per_seq_m):
    tt = min(T, rows)
    bb = 1 if per_seq_m else min(B, rows // tt)
    return bb, tt


def _mod_kernel(cond_ref, w_ref, b_ref, o_ref):
    c = cond_ref[...]
    s = (c * _sigmoid(c)).astype(BF16)
    o_ref[...] = _dot(s, w_ref[...].astype(BF16)) + b_ref[...]


def _modulation(cond, w_mod, b_mod):
    rows = cond.shape[0]
    n = w_mod.shape[1]
    bn = D
    return pl.pallas_call(
        _mod_kernel,
        grid=(n // bn,),
        in_specs=[
            pl.BlockSpec((rows, D), lambda j: (0, 0)),
            pl.BlockSpec((D, bn), lambda j: (0, j)),
            pl.BlockSpec((1, bn), lambda j: (0, j)),
        ],
        out_specs=pl.BlockSpec((rows, bn), lambda j: (0, j)),
        out_shape=jax.ShapeDtypeStruct((rows, n), F32),
        compiler_params=pltpu.CompilerParams(dimension_semantics=("arbitrary",)),
        name="modulation",
    )(cond, w_mod, b_mod.reshape(1, n))


def _stage1_kernel(has_pe, *refs):
    if has_pe:
        x_ref, per_ref, pec_ref, refs = refs[0], refs[1], refs[2], refs[3:]
    else:
        x_ref, refs = refs[0], refs[1:]
    (m_ref, n1_ref, n2_ref, gv_ref, wg_ref, wu_ref, wd_ref, win_ref,
     x1_ref, zx_ref, za_ref) = refs
    m = m_ref[0]
    subs = _subtiles(x_ref)

    xs, hs = [], []
    for b, rs in subs:
        x = x_ref[b, rs, :]
        if has_pe:
            reps = GRID_W // SUBLANES
            r0 = rs.start // GRID_W
            row_half = jnp.concatenate(
                [jnp.tile(per_ref[r * SUBLANES:(r + 1) * SUBLANES, :], (reps, 1))
                 for r in range(r0, r0 + SUB_T // GRID_W)], axis=0)
            col_half = jnp.tile(pec_ref[...], (SUB_T // GRID_W, 1))
            x = x + jnp.concatenate([row_half, col_half], axis=1)
        xs.append(x)
        hs.append(_modulate(x, n1_ref[...], m[0:1], m[1:2]).astype(BF16))

    h2s = []
    for (b, rs), x, h in zip(subs, xs, hs):
        g = _dot(h, wg_ref[...])
        u = _dot(h, wu_ref[...])
        act = (_silu_of_half(g) * u).astype(BF16)
        x1 = x + (0.5 * m[2:3]) * _dot(act, wd_ref[:, 0:D])
        x1_ref[b, rs, :] = x1
        h2s.append(_modulate(x1, n2_ref[...], m[3:4], m[4:5]).astype(BF16))

    branches = ((2, _gelu_of_half), (3, lambda v: _rms(_gelu_of_half(v), gv_ref[...])),
                (4, _sigmoid_of_half), (5, _sigmoid_of_half), (1, _gelu_of_half))
    for (b, rs), h2 in zip(subs, h2s):
        for j, (src, fn) in enumerate(branches):
            z = _dot(h2, win_ref[:, src * D:(src + 1) * D])
            za_ref[b, rs, j * D:(j + 1) * D] = fn(z).astype(BF16)
        zx_ref[b, rs, :] = _dot(h2, win_ref[:, 0:D])


def _stage1(x, pe_tables, m, n1, n2, gv, wg, wu, wd, win):
    B, T, _ = x.shape
    has_pe = pe_tables is not None
    per_batch_m = m.shape[0] > 1
    bb, tt = _token_block(B, T, STAGE1_T, per_batch_m)
    m_map = (lambda t, b: (b, 0, 0)) if per_batch_m else (lambda t, b: (0, 0, 0))
    in_specs = [pl.BlockSpec((bb, tt, D), lambda t, b: (b, t, 0))]
    args = [x]
    if has_pe:
        pe_rows, pe_cols = pe_tables
        in_specs += [pl.BlockSpec((tt // GRID_W * SUBLANES, D // 2), lambda t, b: (t, 0)), _whole()]
        args += [pe_rows, pe_cols]
    in_specs += [pl.BlockSpec((1, N_MOD, D), m_map)] + [_whole()] * 7
    args += [m, n1, n2, gv, wg, wu, wd, win]
    blk = lambda w: pl.BlockSpec((bb, tt, w), lambda t, b: (b, t, 0))
    return pl.pallas_call(
        functools.partial(_stage1_kernel, has_pe),
        grid=(T // tt, B // bb),
        in_specs=in_specs,
        out_specs=[blk(D), blk(D), blk(Z_ACT)],
        out_shape=[jax.ShapeDtypeStruct((B, T, D), F32),
                   jax.ShapeDtypeStruct((B, T, D), F32),
                   jax.ShapeDtypeStruct((B, T, Z_ACT), BF16)],
        compiler_params=pltpu.CompilerParams(
            dimension_semantics=("arbitrary", "arbitrary"),
            vmem_limit_bytes=VMEM_LIMIT),
        name="stage1_ffn_inproj",
    )(*args)


def _gates_to_scratch(blk, xr, wg_ref, bg_ref, lam_ref, a_scr, b_scr):
    lo = blk * GATE_BLOCK
    c4 = (0.5 * LRU_C) * _log_sigmoid(lam_ref[:, lo:lo + GATE_BLOCK])
    t = jnp.tanh(_dot(xr.astype(BF16), wg_ref[blk]) + bg_ref[blk])
    t_r = t[:, 0:GATE_BLOCK]
    t_i = t[:, GATE_BLOCK:2 * GATE_BLOCK]
    log_a = t_r * c4 + c4
    a = jnp.exp(log_a)
    q = (-0.25 * jnp.tanh(log_a)) * (a * a + 1.0)
    half_mult = jnp.where(q > 0.0, q * lax.rsqrt(q), 0.0)
    w = half_mult * xr
    bx = w * t_i + w
    for half in range(GATE_HALVES):
        c = blk * GATE_HALVES + half
        a_scr[c] = a[:, _lane_chunk(half)]
        b_scr[c] = bx[:, _lane_chunk(half)]


def _recurrence(reverse, tt, first, h0_ref, put, a_scr, b_scr, o_scr, carry_scr):
    nb = SCAN_BATCH

    @pl.when(first)
    def _():
        for c in range(N_LANE_CHUNKS):
            carry_scr[c] = h0_ref[:, _lane_chunk(c)]

    n_blocks = tt // SCAN_UNROLL
    order = range(SCAN_UNROLL - 1, -1, -1) if reverse else range(SCAN_UNROLL)

    def block(jb, hs):
        blk = (n_blocks - 1 - jb) if reverse else jb
        base = pl.multiple_of(blk * (SCAN_UNROLL * nb), SCAN_UNROLL * nb)
        hs = list(hs)
        for t in order:
            rows = pl.ds(base + t * nb, nb)
            for c in range(N_LANE_CHUNKS):
                hs[c] = a_scr[c, rows, :] * hs[c] + b_scr[c, rows, :]
                o_scr[c, rows, :] = hs[c]
        return tuple(hs)

    hs = lax.fori_loop(0, n_blocks, block,
                       tuple(carry_scr[c] for c in range(N_LANE_CHUNKS)))
    for c in range(N_LANE_CHUNKS):
        carry_scr[c] = hs[c]
        for s in range(nb):
            put(s, c, o_scr[c, pl.ds(s, tt, stride=nb), :])
    return hs


def _scan_bwd_kernel(zc_ref, zl_ref, zrt_ref, h0_ref, cw_ref, cb_ref, wg_ref, bg_ref, lam_ref,
                     h_ref, xr_ref, ext_scr, a_scr, b_scr, o_scr, carry_scr):
    nb, tt, _ = zc_ref.shape
    j = pl.program_id(1)
    nt = pl.num_programs(1)
    ti = nt - 1 - j
    rows = tt * nb
    lead = CONV_LEFT * nb

    for c in range(N_LANE_CHUNKS):
        cols = _lane_chunk(c)
        for k in range(CONV_LEFT):
            ext_scr[c, k * nb:(k + 1) * nb, :] = jnp.where(
                ti > 0, zl_ref[:, SUBLANES - CONV_LEFT + k, cols], 0.0)
        for k in range(CONV_RIGHT):
            ext_scr[c, lead + rows + k * nb:lead + rows + (k + 1) * nb, :] = jnp.where(
                ti < nt - 1, zrt_ref[:, k, cols], 0.0)
        for s in range(nb):
            ext_scr[c, pl.ds(lead + s, tt, stride=nb), :] = zc_ref[s, :, cols]

    for blk in range(N_GATE_BLOCKS):
        parts = []
        for half in range(GATE_HALVES):
            c = blk * GATE_HALVES + half
            cols = _lane_chunk(c)
            acc = ext_scr[c, 0:rows, :] * cw_ref[0:1, cols] + cb_ref[:, cols]
            for k in range(1, CONV_W):
                acc = acc + ext_scr[c, k * nb:k * nb + rows, :] * cw_ref[k:k + 1, cols]
            parts.append(acc)
        xr = jnp.concatenate(parts, axis=1)
        xr_ref[0, :, blk * GATE_BLOCK:(blk + 1) * GATE_BLOCK] = xr
        _gates_to_scratch(blk, xr, wg_ref, bg_ref, lam_ref, a_scr, b_scr)

    def put(s, c, h):
        h_ref[s, :, _lane_chunk(c)] = h
    _recurrence(True, tt, j == 0, h0_ref, put, a_scr, b_scr, o_scr, carry_scr)


def _scan_fwd_kernel(xr_ref, h0_ref, hb_ref, gr_ref, wg_ref, bg_ref, lam_ref,
                     yrnn_ref, hlast_ref, a_scr, b_scr, o_scr, carry_scr):
    nb, tt, _ = hb_ref.shape
    for blk in range(N_GATE_BLOCKS):
        xr = xr_ref[0, :, blk * GATE_BLOCK:(blk + 1) * GATE_BLOCK]
        _gates_to_scratch(blk, xr, wg_ref, bg_ref, lam_ref, a_scr, b_scr)

    def put(s, c, hf):
        cols = _lane_chunk(c)
        yrnn_ref[s, :, cols] = (gr_ref[s, :, cols].astype(F32) * (hf + hb_ref[s, :, cols])).astype(BF16)
    hs = _recurrence(False, tt, pl.program_id(1) == 0, h0_ref, put, a_scr, b_scr, o_scr, carry_scr)
    for c in range(N_LANE_CHUNKS):
        hlast_ref[:, _lane_chunk(c)] = hs[c]


def _scan_scratch(nb, tt):
    tile = pltpu.VMEM((N_LANE_CHUNKS, tt * nb, LANES), F32)
    return [tile, tile, tile, pltpu.VMEM((N_LANE_CHUNKS, nb, LANES), F32)]


def _scan_bwd(zx, h0, conv_w, conv_b, wgate, bgate, lam):
    B, T, _ = zx.shape
    nb = SCAN_BATCH
    tt = min(T, SCAN_T)
    nt = T // tt
    tpb = tt // SUBLANES
    nhb = T // SUBLANES
    ti = lambda j: nt - 1 - j
    return pl.pallas_call(
        _scan_bwd_kernel,
        grid=(B // nb, nt),
        in_specs=[
            pl.BlockSpec((nb, tt, D), lambda g, j: (g, ti(j), 0)),
            pl.BlockSpec((nb, SUBLANES, D),
                         lambda g, j: (g, jnp.maximum(ti(j) * tpb - 1, 0), 0)),
            pl.BlockSpec((nb, SUBLANES, D),
                         lambda g, j: (g, jnp.minimum((ti(j) + 1) * tpb, nhb - 1), 0)),
            pl.BlockSpec((nb, D), lambda g, j: (g, 0)),
            _whole(), _whole(), _whole(), _whole(), _whole(),
        ],
        out_specs=[pl.BlockSpec((nb, tt, D), lambda g, j: (g, ti(j), 0)),
                   pl.BlockSpec((1, tt * nb, D), lambda g, j: (g, ti(j), 0))],
        out_shape=[jax.ShapeDtypeStruct((B, T, D), F32),
                   jax.ShapeDtypeStruct((B // nb, T * nb, D), F32)],
        scratch_shapes=[pltpu.VMEM((N_LANE_CHUNKS, (tt + CONV_W - 1) * nb, LANES), F32)]
        + _scan_scratch(nb, tt),
        compiler_params=pltpu.CompilerParams(
            dimension_semantics=("arbitrary", "arbitrary"),
            vmem_limit_bytes=VMEM_LIMIT),
        name="scan_bwd",
    )(zx, zx, zx, h0, conv_w, conv_b, wgate, bgate, lam)


def _scan_fwd(xr_tm, h0, hb, za, wgate, bgate, lam):
    B, T, _ = hb.shape
    nb = SCAN_BATCH
    tt = min(T, SCAN_T)
    tok = pl.BlockSpec((nb, tt, D), lambda g, j: (g, j, 0))
    return pl.pallas_call(
        _scan_fwd_kernel,
        grid=(B // nb, T // tt),
        in_specs=[
            pl.BlockSpec((1, tt * nb, D), lambda g, j: (g, j, 0)),
            pl.BlockSpec((nb, D), lambda g, j: (g, 0)),
            tok,
            pl.BlockSpec((nb, tt, D), lambda g, j: (g, j, ZA_RNN_GATE)),
            _whole(), _whole(), _whole(),
        ],
        out_specs=[tok, pl.BlockSpec((nb, D), lambda g, j: (g, 0))],
        out_shape=[jax.ShapeDtypeStruct((B, T, D), BF16),
                   jax.ShapeDtypeStruct((B, D), F32)],
        scratch_shapes=_scan_scratch(nb, tt),
        compiler_params=pltpu.CompilerParams(
            dimension_semantics=("arbitrary", "arbitrary"),
            vmem_limit_bytes=VMEM_LIMIT),
        name="scan_fwd",
    )(xr_tm, h0, hb, za, wgate, bgate, lam)


def _stage3_kernel(x1_ref, yrnn_ref, za_ref, m_ref, ws_ref, bs_ref,
                   wbr_ref, wbg_ref, wo_ref, n3_ref, wg_ref, wu_ref, wd_ref, nf_ref,
                   y_ref, mixed_scr):
    m = m_ref[0]
    subs = _subtiles(x1_ref)

    x2s, hs = [], []
    for i, (b, rs) in enumerate(subs):
        zcol = lambda c: za_ref[b, rs, c * D:(c + 1) * D]
        rnn_proj = _dot(yrnn_ref[b, rs, :], wbr_ref[:, 0:D])
        n_chunks = SUB_T // CHUNK
        for g in range(GROUPS):
            cs = _lane_chunk(g)
            vn_g = jnp.concatenate(
                [za_ref[b, rs.start + n * CHUNK:rs.start + (n + 1) * CHUNK, D + g * LANES:D + (g + 1) * LANES]
                 for n in range(n_chunks)], axis=1)
            mix_g = _dot(ws_ref[g], vn_g)
            for n in range(n_chunks):
                mixed_scr[i, n * CHUNK:(n + 1) * CHUNK, cs] = mix_g[:, _lane_chunk(n)] + bs_ref[:, cs]
        y_g = zcol(0).astype(F32) * mixed_scr[i]
        g_proj = _dot(y_g.astype(BF16), wbg_ref[:, 0:D])
        merged = zcol(2).astype(F32) * rnn_proj + zcol(3).astype(F32) * g_proj
        x2 = x1_ref[b, rs, :] + m[5:6] * _dot(merged.astype(BF16), wo_ref[:, 0:D])
        x2s.append(x2)
        hs.append(_modulate(x2, n3_ref[...], m[6:7], m[7:8]).astype(BF16))

    for (b, rs), x2, h in zip(subs, x2s, hs):
        g = _dot(h, wg_ref[...])
        uu = _dot(h, wu_ref[...])
        act = (_silu_of_half(g) * uu).astype(BF16)
        x3 = x2 + (0.5 * m[8:9]) * _dot(act, wd_ref[:, 0:D])
        y_ref[b, rs, :] = _rms(x3, nf_ref[...])


def _stage3(x1, y_rnn, za, m, ws, bs, wbr, wbg, wo, n3, wg, wu, wd, nf):
    B, T, _ = x1.shape
    per_batch_m = m.shape[0] > 1
    bb, tt = _token_block(B, T, STAGE3_T, per_batch_m)
    m_map = (lambda t, b: (b, 0, 0)) if per_batch_m else (lambda t, b: (0, 0, 0))
    blk = lambda w: pl.BlockSpec((bb, tt, w), lambda t, b: (b, t, 0))
    return pl.pallas_call(
        _stage3_kernel,
        grid=(T // tt, B // bb),
        in_specs=[blk(D), blk(D), blk(ZA_MIX),
                  pl.BlockSpec((1, N_MOD, D), m_map)] + [_whole()] * 10,
        out_specs=blk(D),
        out_shape=jax.ShapeDtypeStruct((B, T, D), F32),
        scratch_shapes=[pltpu.VMEM((bb * tt // SUB_T, SUB_T, D), F32)],
        compiler_params=pltpu.CompilerParams(
            dimension_semantics=("arbitrary", "arbitrary"),
            vmem_limit_bytes=VMEM_LIMIT),
        name="stage3_mix_ffn",
    )(x1, y_rnn, za, m, ws, bs, wbr, wbg, wo, n3, wg, wu, wd, nf)


def _grid_pos_tables(n_tokens):
    q = D // 4
    freqs = 1.0 / (10000.0 ** (jnp.arange(q, dtype=F32) / q))
    half = lambda idx: jnp.concatenate(
        [jnp.sin(idx[:, None] * freqs), jnp.cos(idx[:, None] * freqs)], axis=-1)
    rows = half(jnp.arange(n_tokens // GRID_W).astype(F32))
    cols = half(jnp.arange(GRID_W).astype(F32))
    return jnp.repeat(rows, SUBLANES, axis=0), cols


def _gate_weights(w_r, w_i, b_r, b_i):
    def bd(w):
        w4 = w.reshape(N_GATE_BLOCKS, HEADS_PER_BLOCK, HEAD_DIM, HEAD_DIM)
        eye = jnp.eye(HEADS_PER_BLOCK, dtype=w.dtype)
        return jnp.einsum('ghij,hk->ghikj', w4, eye).reshape(N_GATE_BLOCKS, GATE_BLOCK, GATE_BLOCK)
    wg = (0.5 * jnp.concatenate([bd(w_r), bd(w_i)], axis=-1)).astype(BF16)
    bg = 0.5 * jnp.concatenate([b_r.reshape(N_GATE_BLOCKS, 1, GATE_BLOCK),
                                b_i.reshape(N_GATE_BLOCKS, 1, GATE_BLOCK)], axis=-1)
    return wg, bg


def kernel(x_prompt, x_sample, state_rnn_fwd, state_rnn_bwd, c, c_ctx, w_mod, b_mod, norm1, norm2, norm3, ff1_gate, ff1_up, ff1_down, w_in, conv_w, conv_b, w_r, b_r, w_i, b_i, lam, gmlp_norm, w_s, b_s, w_br, w_bg, w_out, ff2_gate, ff2_up, ff2_down, norm_f):
    l = 0
    nbatch_lat = c.shape[0]
    cond_rows = 2 * SUBLANES
    cond = jnp.zeros((cond_rows, D), F32).at[:nbatch_lat].set(c).at[nbatch_lat].set(c_ctx)
    m_all = _modulation(cond, w_mod[l], b_mod[l])
    m_lat = m_all[:nbatch_lat].reshape(nbatch_lat, N_MOD, D)
    m_ctx = m_all[nbatch_lat:nbatch_lat + 1].reshape(1, N_MOD, D)

    row = lambda v: v.reshape(1, D)
    bf = lambda w: w.astype(BF16)

    def bfp(w):
        assert (w.shape[-1] // LANES) % 2 == 0
        return jnp.pad(w.astype(BF16), ((0, 0), (0, LANES)))
    n1, n2, n3, nf, gv = row(norm1[l]), row(norm2[l]), row(norm3[l]), row(norm_f), row(gmlp_norm[l])
    wg1, wu1, wd1 = bf(0.5 * ff1_gate[l]), bf(ff1_up[l]), bfp(ff1_down[l])
    wg2, wu2, wd2 = bf(0.5 * ff2_gate[l]), bf(ff2_up[l]), bfp(ff2_down[l])
    in_scale = jnp.concatenate([jnp.ones((D,), F32), jnp.full((N_BRANCH * D,), 0.5, F32)])
    win, wbr, wbg, wo = bfp(w_in[l] * in_scale), bfp(w_br[l]), bfp(w_bg[l]), bfp(w_out[l])
    ws = bf(w_s[l])
    bs = jnp.repeat(b_s[l].T, D // GROUPS, axis=1)
    cw, cb, lam_l = conv_w[l], row(conv_b[l]), lam[l]
    (wgt_f, bgt_f), (wgt_b, bgt_b) = [
        _gate_weights(w_r[l, d], w_i[l, d], b_r[l, d], b_i[l, d]) for d in range(2)]

    def trunk(x, pe_tables, m, h0_f, h0_b):
        x1, zx, za = _stage1(x, pe_tables, m, n1, n2, gv, wg1, wu1, wd1, win)
        hb, xr_tm = _scan_bwd(zx, h0_b, cw, cb, wgt_b, bgt_b, row(lam_l[1]))
        y_rnn, hf_last = _scan_fwd(xr_tm, h0_f, hb, za, wgt_f, bgt_f, row(lam_l[0]))
        y = _stage3(x1, y_rnn, za, m, ws, bs, wbr, wbg, wo, n3, wg2, wu2, wd2, nf)
        return y, hf_last, hb

    zeros = jnp.zeros((x_prompt.shape[0], D), F32)
    y_prompt, hf_last, hb_c = trunk(x_prompt, None, m_ctx, zeros, zeros)
    new_f = hf_last[:, None, :]
    new_b = hb_c[:, 0:1, :]

    y_sample, _, _ = trunk(x_sample, _grid_pos_tables(x_sample.shape[1]), m_lat,
                           state_rnn_fwd[:, l], state_rnn_bwd[:, l])
    return (y_prompt, y_sample, new_f, new_b)
```

```python
import functools

import jax
import jax.numpy as jnp
from jax import lax
from jax.experimental import pallas as pl
from jax.experimental.pallas import tpu as pltpu

D = 1024
D_FF = 2816
N_MOD = 9
EPS = 1e-6
CHUNK = 128
GROUPS = 8
HEADS = 16
HEAD_DIM = 64
CONV_W = 4
CONV_LEFT = 2
CONV_RIGHT = CONV_W - 1 - CONV_LEFT
LRU_C = 8.0
GRID_W = 64
N_BRANCH = 5
Z_ACT = N_BRANCH * D
ZA_RNN_GATE = 4

LANES = 128
SUBLANES = 8
N_LANE_CHUNKS = D // LANES
HEADS_PER_BLOCK = 4
GATE_BLOCK = HEADS_PER_BLOCK * HEAD_DIM
N_GATE_BLOCKS = D // GATE_BLOCK
GATE_HALVES = GATE_BLOCK // LANES

SUB_T = 256
STAGE1_T = 512
STAGE3_T = 512
SCAN_BATCH = SUBLANES
SCAN_T = 128
SCAN_UNROLL = 8
VMEM_LIMIT = 61 * 1024 * 1024

F32 = jnp.float32
BF16 = jnp.bfloat16


def _rms(x, g):
    ms = jnp.mean(x * x, axis=-1, keepdims=True)
    return (x * lax.rsqrt(ms + EPS)) * g


def _modulate(x, g, shift, scale):
    return _rms(x, g * (1.0 + scale)) + shift


def _sigmoid(x):
    return 0.5 * jnp.tanh(0.5 * x) + 0.5


def _sigmoid_of_half(hx):
    return 0.5 * jnp.tanh(hx) + 0.5


def _silu_of_half(hx):
    return hx * jnp.tanh(hx) + hx


GELU_K1 = 2.0 * 0.7978845608028654
GELU_K2 = 4.0 * GELU_K1 * 0.044715


def _gelu_of_half(hx):
    return hx * jnp.tanh(hx * (GELU_K1 + GELU_K2 * (hx * hx))) + hx


def _log_sigmoid(x):
    return jnp.minimum(x, 0.0) - jnp.log1p(jnp.exp(-jnp.abs(x)))


def _dot(a, b):
    return jnp.dot(a, b, preferred_element_type=F32)


def _whole(memory_space=pltpu.VMEM):
    return pl.BlockSpec(memory_space=memory_space)


def _lane_chunk(c):
    return slice(c * LANES, (c + 1) * LANES)


def _subtiles(ref):
    bb, tt = ref.shape[0], ref.shape[1]
    return [(b, slice(s * SUB_T, (s + 1) * SUB_T)) for b in range(bb) for s in range(tt // SUB_T)]


def _token_block(B, T, rows, per_seq_m):
    tt = min(T, rows)
    bb = 1 if per_seq_m else min(B, rows // tt)
    return bb, tt


def _mod_kernel(cond_ref, w_ref, b_ref, o_ref):
    c = cond_ref[...]
    s = (c * _sigmoid(c)).astype(BF16)
    o_ref[...] = _dot(s, w_ref[...].astype(BF16)) + b_ref[...]


def _modulation(cond, w_mod, b_mod):
    rows = cond.shape[0]
    n = w_mod.shape[1]
    bn = D
    return pl.pallas_call(
        _mod_kernel,
        grid=(n // bn,),
        in_specs=[
            pl.BlockSpec((rows, D), lambda j: (0, 0)),
            pl.BlockSpec((D, bn), lambda j: (0, j)),
            pl.BlockSpec((1, bn), lambda j: (0, j)),
        ],
        out_specs=pl.BlockSpec((rows, bn), lambda j: (0, j)),
        out_shape=jax.ShapeDtypeStruct((rows, n), F32),
        compiler_params=pltpu.CompilerParams(dimension_semantics=("arbitrary",)),
        name="modulation",
    )(cond, w_mod, b_mod.reshape(1, n))


def _stage1_kernel(has_pe, *refs):
    if has_pe:
        x_ref, per_ref, pec_ref, refs = refs[0], refs[1], refs[2], refs[3:]
    else:
        x_ref, refs = refs[0], refs[1:]
    (m_ref, n1_ref, n2_ref, gv_ref, wg_ref, wu_ref, wd_ref, win_ref,
     x1_ref, zx_ref, za_ref) = refs
    m = m_ref[0]
    subs = _subtiles(x_ref)

    xs, hs = [], []
    for b, rs in subs:
        x = x_ref[b, rs, :]
        if has_pe:
            reps = GRID_W // SUBLANES
            r0 = rs.start // GRID_W
            row_half = jnp.concatenate(
                [jnp.tile(per_ref[r * SUBLANES:(r + 1) * SUBLANES, :], (reps, 1))
                 for r in range(r0, r0 + SUB_T // GRID_W)], axis=0)
            col_half = jnp.tile(pec_ref[...], (SUB_T // GRID_W, 1))
            x = x + jnp.concatenate([row_half, col_half], axis=1)
        xs.append(x)
        hs.append(_modulate(x, n1_ref[...], m[0:1], m[1:2]).astype(BF16))

    h2s = []
    for (b, rs), x, h in zip(subs, xs, hs):
        g = _dot(h, wg_ref[...])
        u = _dot(h, wu_ref[...])
        act = (_silu_of_half(g) * u).astype(BF16)
        x1 = x + (0.5 * m[2:3]) * _dot(act, wd_ref[:, 0:D])
        x1_ref[b, rs, :] = x1
        h2s.append(_modulate(x1, n2_ref[...], m[3:4], m[4:5]).astype(BF16))

    branches = ((2, _gelu_of_half), (3, lambda v: _rms(_gelu_of_half(v), gv_ref[...])),
                (4, _sigmoid_of_half), (5, _sigmoid_of_half), (1, _gelu_of_half))
    for (b, rs), h2 in zip(subs, h2s):
        for j, (src, fn) in enumerate(branches):
            z = _dot(h2, win_ref[:, src * D:(src + 1) * D])
            za_ref[b, rs, j * D:(j + 1) * D] = fn(z).astype(BF16)
        zx_ref[b, rs, :] = _dot(h2, win_ref[:, 0:D])


def _stage1(x, pe_tables, m, n1, n2, gv, wg, wu, wd, win):
    B, T, _ = x.shape
    has_pe = pe_tables is not None
    per_batch_m = m.shape[0] > 1
    bb, tt = _token_block(B, T, STAGE1_T, per_batch_m)
    m_map = (lambda t, b: (b, 0, 0)) if per_batch_m else (lambda t, b: (0, 0, 0))
    in_specs = [pl.BlockSpec((bb, tt, D), lambda t, b: (b, t, 0))]
    args = [x]
    if has_pe:
        pe_rows, pe_cols = pe_tables
        in_specs += [pl.BlockSpec((tt // GRID_W * SUBLANES, D // 2), lambda t, b: (t, 0)), _whole()]
        args += [pe_rows, pe_cols]
    in_specs += [pl.BlockSpec((1, N_MOD, D), m_map)] + [_whole()] * 7
    args += [m, n1, n2, gv, wg, wu, wd, win]
    blk = lambda w: pl.BlockSpec((bb, tt, w), lambda t, b: (b, t, 0))
    return pl.pallas_call(
        functools.partial(_stage1_kernel, has_pe),
        grid=(T // tt, B // bb),
        in_specs=in_specs,
        out_specs=[blk(D), blk(D), blk(Z_ACT)],
        out_shape=[jax.ShapeDtypeStruct((B, T, D), F32),
                   jax.ShapeDtypeStruct((B, T, D), F32),
                   jax.ShapeDtypeStruct((B, T, Z_ACT), BF16)],
        compiler_params=pltpu.CompilerParams(
            dimension_semantics=("arbitrary", "arbitrary"),
            vmem_limit_bytes=VMEM_LIMIT),
        name="stage1_ffn_inproj",
    )(*args)


def _gates_to_scratch(blk, xr, wg_ref, bg_ref, lam_ref, a_scr, b_scr):
    lo = blk * GATE_BLOCK
    c4 = (0.5 * LRU_C) * _log_sigmoid(lam_ref[:, lo:lo + GATE_BLOCK])
    t = jnp.tanh(_dot(xr.astype(BF16), wg_ref[blk]) + bg_ref[blk])
    t_r = t[:, 0:GATE_BLOCK]
    t_i = t[:, GATE_BLOCK:2 * GATE_BLOCK]
    log_a = t_r * c4 + c4
    a = jnp.exp(log_a)
    q = (-0.25 * jnp.tanh(log_a)) * (a * a + 1.0)
    half_mult = jnp.where(q > 0.0, q * lax.rsqrt(q), 0.0)
    w = half_mult * xr
    bx = w * t_i + w
    for half in range(GATE_HALVES):
        c = blk * GATE_HALVES + half
        a_scr[c] = a[:, _lane_chunk(half)]
        b_scr[c] = bx[:, _lane_chunk(half)]


def _recurrence(reverse, tt, first, h0_ref, emit, a_scr, b_scr, carry_scr):
    nb = SCAN_BATCH

    @pl.when(first)
    def _():
        for c in range(N_LANE_CHUNKS):
            carry_scr[c] = h0_ref[:, _lane_chunk(c)]

    n_blocks = tt // SCAN_UNROLL
    order = range(SCAN_UNROLL - 1, -1, -1) if reverse else range(SCAN_UNROLL)

    def block(jb, hs):
        blk = (n_blocks - 1 - jb) if reverse else jb
        base = pl.multiple_of(blk * (SCAN_UNROLL * nb), SCAN_UNROLL * nb)
        hs = list(hs)
        for t in order:
            rows = pl.ds(base + t * nb, nb)
            for c in range(N_LANE_CHUNKS):
                hs[c] = a_scr[c, rows, :] * hs[c] + b_scr[c, rows, :]
                emit(rows, c, hs[c])
        return tuple(hs)

    hs = lax.fori_loop(0, n_blocks, block,
                       tuple(carry_scr[c] for c in range(N_LANE_CHUNKS)))
    for c in range(N_LANE_CHUNKS):
        carry_scr[c] = hs[c]
    return hs


def _scan_bwd_kernel(zc_ref, zl_ref, zrt_ref, h0_ref, cw_ref, cb_ref, wg_ref, bg_ref, lam_ref,
                     h_ref, xr_ref, ext_scr, a_scr, b_scr, carry_scr):
    nb, tt, _ = zc_ref.shape
    j = pl.program_id(1)
    nt = pl.num_programs(1)
    ti = nt - 1 - j
    rows = tt * nb
    lead = CONV_LEFT * nb

    for c in range(N_LANE_CHUNKS):
        cols = _lane_chunk(c)
        for k in range(CONV_LEFT):
            ext_scr[c, k * nb:(k + 1) * nb, :] = jnp.where(
                ti > 0, zl_ref[:, SUBLANES - CONV_LEFT + k, cols], 0.0)
        for k in range(CONV_RIGHT):
            ext_scr[c, lead + rows + k * nb:lead + rows + (k + 1) * nb, :] = jnp.where(
                ti < nt - 1, zrt_ref[:, k, cols], 0.0)
        for s in range(nb):
            ext_scr[c, pl.ds(lead + s, tt, stride=nb), :] = zc_ref[s, :, cols]

    for blk in range(N_GATE_BLOCKS):
        parts = []
        for half in range(GATE_HALVES):
            c = blk * GATE_HALVES + half
            cols = _lane_chunk(c)
            acc = ext_scr[c, 0:rows, :] * cw_ref[0:1, cols] + cb_ref[:, cols]
            for k in range(1, CONV_W):
                acc = acc + ext_scr[c, k * nb:k * nb + rows, :] * cw_ref[k:k + 1, cols]
            parts.append(acc)
        xr = jnp.concatenate(parts, axis=1)
        xr_ref[0, :, blk * GATE_BLOCK:(blk + 1) * GATE_BLOCK] = xr
        _gates_to_scratch(blk, xr, wg_ref, bg_ref, lam_ref, a_scr, b_scr)

    def emit(rows, c, h):
        h_ref[0, rows, _lane_chunk(c)] = h
    _recurrence(True, tt, j == 0, h0_ref, emit, a_scr, b_scr, carry_scr)


def _scan_fwd_kernel(xr_ref, h0_ref, hb_ref, wg_ref, bg_ref, lam_ref,
                     hsum_ref, hlast_ref, a_scr, b_scr, o_scr, carry_scr):
    nb, tt, _ = hsum_ref.shape
    for blk in range(N_GATE_BLOCKS):
        xr = xr_ref[0, :, blk * GATE_BLOCK:(blk + 1) * GATE_BLOCK]
        _gates_to_scratch(blk, xr, wg_ref, bg_ref, lam_ref, a_scr, b_scr)

    def emit(rows, c, hf):
        o_scr[c, rows, :] = hf + hb_ref[0, rows, _lane_chunk(c)]
    hs = _recurrence(False, tt, pl.program_id(1) == 0, h0_ref, emit, a_scr, b_scr, carry_scr)
    for c in range(N_LANE_CHUNKS):
        hlast_ref[:, _lane_chunk(c)] = hs[c]
        for s in range(nb):
            hsum_ref[s, :, _lane_chunk(c)] = o_scr[c, pl.ds(s, tt, stride=nb), :]


def _scan_scratch(nb, tt, n_tiles):
    tile = pltpu.VMEM((N_LANE_CHUNKS, tt * nb, LANES), F32)
    return [tile] * n_tiles + [pltpu.VMEM((N_LANE_CHUNKS, nb, LANES), F32)]


def _scan_bwd(zx, h0, conv_w, conv_b, wgate, bgate, lam):
    B, T, _ = zx.shape
    nb = SCAN_BATCH
    tt = min(T, SCAN_T)
    nt = T // tt
    tpb = tt // SUBLANES
    nhb = T // SUBLANES
    ti = lambda j: nt - 1 - j
    return pl.pallas_call(
        _scan_bwd_kernel,
        grid=(B // nb, nt),
        in_specs=[
            pl.BlockSpec((nb, tt, D), lambda g, j: (g, ti(j), 0)),
            pl.BlockSpec((nb, SUBLANES, D),
                         lambda g, j: (g, jnp.maximum(ti(j) * tpb - 1, 0), 0)),
            pl.BlockSpec((nb, SUBLANES, D),
                         lambda g, j: (g, jnp.minimum((ti(j) + 1) * tpb, nhb - 1), 0)),
            pl.BlockSpec((nb, D), lambda g, j: (g, 0)),
            _whole(), _whole(), _whole(), _whole(), _whole(),
        ],
        out_specs=[pl.BlockSpec((1, tt * nb, D), lambda g, j: (g, ti(j), 0))] * 2,
        out_shape=[jax.ShapeDtypeStruct((B // nb, T * nb, D), F32)] * 2,
        scratch_shapes=[pltpu.VMEM((N_LANE_CHUNKS, (tt + CONV_W - 1) * nb, LANES), F32)]
        + _scan_scratch(nb, tt, 2),
        compiler_params=pltpu.CompilerParams(
            dimension_semantics=("arbitrary", "arbitrary"),
            vmem_limit_bytes=VMEM_LIMIT),
        name="scan_bwd",
    )(zx, zx, zx, h0, conv_w, conv_b, wgate, bgate, lam)


def _scan_fwd(xr_tm, hb_tm, h0, wgate, bgate, lam):
    G, rows_total, _ = xr_tm.shape
    nb = SCAN_BATCH
    T = rows_total // nb
    tt = min(T, SCAN_T)
    tm_tile = pl.BlockSpec((1, tt * nb, D), lambda g, j: (g, j, 0))
    return pl.pallas_call(
        _scan_fwd_kernel,
        grid=(G, T // tt),
        in_specs=[
            tm_tile,
            pl.BlockSpec((nb, D), lambda g, j: (g, 0)),
            tm_tile,
            _whole(), _whole(), _whole(),
        ],
        out_specs=[pl.BlockSpec((nb, tt, D), lambda g, j: (g, j, 0)),
                   pl.BlockSpec((nb, D), lambda g, j: (g, 0))],
        out_shape=[jax.ShapeDtypeStruct((G * nb, T, D), F32),
                   jax.ShapeDtypeStruct((G * nb, D), F32)],
        scratch_shapes=_scan_scratch(nb, tt, 3),
        compiler_params=pltpu.CompilerParams(
            dimension_semantics=("arbitrary", "arbitrary"),
            vmem_limit_bytes=VMEM_LIMIT),
        name="scan_fwd",
    )(xr_tm, h0, hb_tm, wgate, bgate, lam)


def _stage3_kernel(x1_ref, hsum_ref, za_ref, m_ref, ws_ref, bs_ref,
                   wbr_ref, wbg_ref, wo_ref, n3_ref, wg_ref, wu_ref, wd_ref, nf_ref,
                   y_ref, mixed_scr):
    m = m_ref[0]
    subs = _subtiles(x1_ref)

    x2s, hs = [], []
    for i, (b, rs) in enumerate(subs):
        zcol = lambda c: za_ref[b, rs, c * D:(c + 1) * D]
        y_rnn = zcol(ZA_RNN_GATE).astype(F32) * hsum_ref[b, rs, :]
        rnn_proj = _dot(y_rnn.astype(BF16), wbr_ref[:, 0:D])
        n_chunks = SUB_T // CHUNK
        for g in range(GROUPS):
            cs = _lane_chunk(g)
            vn_g = jnp.concatenate(
                [za_ref[b, rs.start + n * CHUNK:rs.start + (n + 1) * CHUNK, D + g * LANES:D + (g + 1) * LANES]
                 for n in range(n_chunks)], axis=1)
            mix_g = _dot(ws_ref[g], vn_g)
            for n in range(n_chunks):
                mixed_scr[i, n * CHUNK:(n + 1) * CHUNK, cs] = mix_g[:, _lane_chunk(n)] + bs_ref[:, cs]
        y_g = zcol(0).astype(F32) * mixed_scr[i]
        g_proj = _dot(y_g.astype(BF16), wbg_ref[:, 0:D])
        merged = zcol(2).astype(F32) * rnn_proj + zcol(3).astype(F32) * g_proj
        x2 = x1_ref[b, rs, :] + m[5:6] * _dot(merged.astype(BF16), wo_ref[:, 0:D])
        x2s.append(x2)
        hs.append(_modulate(x2, n3_ref[...], m[6:7], m[7:8]).astype(BF16))

    for (b, rs), x2, h in zip(subs, x2s, hs):
        g = _dot(h, wg_ref[...])
        uu = _dot(h, wu_ref[...])
        act = (_silu_of_half(g) * uu).astype(BF16)
        x3 = x2 + (0.5 * m[8:9]) * _dot(act, wd_ref[:, 0:D])
        y_ref[b, rs, :] = _rms(x3, nf_ref[...])


def _stage3(x1, h_sum, za, m, ws, bs, wbr, wbg, wo, n3, wg, wu, wd, nf):
    B, T, _ = x1.shape
    per_batch_m = m.shape[0] > 1
    bb, tt = _token_block(B, T, STAGE3_T, per_batch_m)
    m_map = (lambda t, b: (b, 0, 0)) if per_batch_m else (lambda t, b: (0, 0, 0))
    blk = lambda w: pl.BlockSpec((bb, tt, w), lambda t, b: (b, t, 0))
    return pl.pallas_call(
        _stage3_kernel,
        grid=(T // tt, B // bb),
        in_specs=[blk(D), blk(D), blk(Z_ACT),
                  pl.BlockSpec((1, N_MOD, D), m_map)] + [_whole()] * 10,
        out_specs=blk(D),
        out_shape=jax.ShapeDtypeStruct((B, T, D), F32),
        scratch_shapes=[pltpu.VMEM((bb * tt // SUB_T, SUB_T, D), F32)],
        compiler_params=pltpu.CompilerParams(
            dimension_semantics=("arbitrary", "arbitrary"),
            vmem_limit_bytes=VMEM_LIMIT),
        name="stage3_mix_ffn",
    )(x1, h_sum, za, m, ws, bs, wbr, wbg, wo, n3, wg, wu, wd, nf)


def _grid_pos_tables(n_tokens):
    q = D // 4
    freqs = 1.0 / (10000.0 ** (jnp.arange(q, dtype=F32) / q))
    half = lambda idx: jnp.concatenate(
        [jnp.sin(idx[:, None] * freqs), jnp.cos(idx[:, None] * freqs)], axis=-1)
    rows = half(jnp.arange(n_tokens // GRID_W).astype(F32))
    cols = half(jnp.arange(GRID_W).astype(F32))
    return jnp.repeat(rows, SUBLANES, axis=0), cols


def _gate_weights(w_r, w_i, b_r, b_i):
    def bd(w):
        w4 = w.reshape(N_GATE_BLOCKS, HEADS_PER_BLOCK, HEAD_DIM, HEAD_DIM)
        eye = jnp.eye(HEADS_PER_BLOCK, dtype=w.dtype)
        return jnp.einsum('ghij,hk->ghikj', w4, eye).reshape(N_GATE_BLOCKS, GATE_BLOCK, GATE_BLOCK)
    wg = (0.5 * jnp.concatenate([bd(w_r), bd(w_i)], axis=-1)).astype(BF16)
    bg = 0.5 * jnp.concatenate([b_r.reshape(N_GATE_BLOCKS, 1, GATE_BLOCK),
                                b_i.reshape(N_GATE_BLOCKS, 1, GATE_BLOCK)], axis=-1)
    return wg, bg


def kernel(x_prompt, x_sample, state_rnn_fwd, state_rnn_bwd, c, c_ctx, w_mod, b_mod, norm1, norm2, norm3, ff1_gate, ff1_up, ff1_down, w_in, conv_w, conv_b, w_r, b_r, w_i, b_i, lam, gmlp_norm, w_s, b_s, w_br, w_bg, w_out, ff2_gate, ff2_up, ff2_down, norm_f):
    l = 0
    nbatch_lat = c.shape[0]
    cond_rows = 2 * SUBLANES
    cond = jnp.zeros((cond_rows, D), F32).at[:nbatch_lat].set(c).at[nbatch_lat].set(c_ctx)
    m_all = _modulation(cond, w_mod[l], b_mod[l])
    m_lat = m_all[:nbatch_lat].reshape(nbatch_lat, N_MOD, D)
    m_ctx = m_all[nbatch_lat:nbatch_lat + 1].reshape(1, N_MOD, D)

    row = lambda v: v.reshape(1, D)
    bf = lambda w: w.astype(BF16)

    def bfp(w):
        assert (w.shape[-1] // LANES) % 2 == 0
        return jnp.pad(w.astype(BF16), ((0, 0), (0, LANES)))
    n1, n2, n3, nf, gv = row(norm1[l]), row(norm2[l]), row(norm3[l]), row(norm_f), row(gmlp_norm[l])
    wg1, wu1, wd1 = bf(0.5 * ff1_gate[l]), bf(ff1_up[l]), bfp(ff1_down[l])
    wg2, wu2, wd2 = bf(0.5 * ff2_gate[l]), bf(ff2_up[l]), bfp(ff2_down[l])
    in_scale = jnp.concatenate([jnp.ones((D,), F32), jnp.full((N_BRANCH * D,), 0.5, F32)])
    win, wbr, wbg, wo = bfp(w_in[l] * in_scale), bfp(w_br[l]), bfp(w_bg[l]), bfp(w_out[l])
    ws = bf(w_s[l])
    bs = jnp.repeat(b_s[l].T, D // GROUPS, axis=1)
    cw, cb, lam_l = conv_w[l], row(conv_b[l]), lam[l]
    (wgt_f, bgt_f), (wgt_b, bgt_b) = [
        _gate_weights(w_r[l, d], w_i[l, d], b_r[l, d], b_i[l, d]) for d in range(2)]

    def trunk(x, pe_tables, m, h0_f, h0_b):
        x1, zx, za = _stage1(x, pe_tables, m, n1, n2, gv, wg1, wu1, wd1, win)
        hb_tm, xr_tm = _scan_bwd(zx, h0_b, cw, cb, wgt_b, bgt_b, row(lam_l[1]))
        h_sum, hf_last = _scan_fwd(xr_tm, hb_tm, h0_f, wgt_f, bgt_f, row(lam_l[0]))
        y = _stage3(x1, h_sum, za, m, ws, bs, wbr, wbg, wo, n3, wg2, wu2, wd2, nf)
        return y, hf_last, hb_tm[:, :SCAN_BATCH, :].reshape(-1, 1, D)

    zeros = jnp.zeros((x_prompt.shape[0], D), F32)
    y_prompt, hf_last, new_b = trunk(x_prompt, None, m_ctx, zeros, zeros)
    new_f = hf_last[:, None, :]

    y_sample, _, _ = trunk(x_sample, _grid_pos_tables(x_sample.shape[1]), m_lat,
                           state_rnn_fwd[:, l], state_rnn_bwd[:, l])
    return (y_prompt, y_sample, new_f, new_b)
```

```python
import functools

import jax
import jax.numpy as jnp
from jax import lax
from jax.experimental import pallas as pl
from jax.experimental.pallas import tpu as pltpu

D = 1024
D_FF = 2816
N_MOD = 9
EPS = 1e-6
CHUNK = 128
GROUPS = 8
HEADS = 16
HEAD_DIM = 64
CONV_W = 4
CONV_LEFT = 2
CONV_RIGHT = CONV_W - 1 - CONV_LEFT
LRU_C = 8.0
GRID_W = 64
N_BRANCH = 5
Z_ACT = N_BRANCH * D
ZA_RNN_GATE = 4

LANES = 128
SUBLANES = 8
N_LANE_CHUNKS = D // LANES
HEADS_PER_BLOCK = 4
GATE_BLOCK = HEADS_PER_BLOCK * HEAD_DIM
N_GATE_BLOCKS = D // GATE_BLOCK
GATE_HALVES = GATE_BLOCK // LANES

SUB_T = 256
STAGE1_T = 512
STAGE3_T = 512
SCAN_BATCH = SUBLANES
SCAN_T = 128
SCAN_UNROLL = 8
VMEM_LIMIT = 61 * 1024 * 1024

F32 = jnp.float32
BF16 = jnp.bfloat16


def _rms(x, g):
    ms = jnp.mean(x * x, axis=-1, keepdims=True)
    return (x * lax.rsqrt(ms + EPS)) * g


def _modulate(x, g, shift, scale):
    return _rms(x, g * (1.0 + scale)) + shift


def _sigmoid(x):
    return 0.5 * jnp.tanh(0.5 * x) + 0.5


def _sigmoid_of_half(hx):
    return 0.5 * jnp.tanh(hx) + 0.5


def _silu_of_half(hx):
    return hx * jnp.tanh(hx) + hx


GELU_K1 = 2.0 * 0.7978845608028654
GELU_K2 = 4.0 * GELU_K1 * 0.044715


def _gelu_of_half(hx):
    return hx * jnp.tanh(hx * (GELU_K1 + GELU_K2 * (hx * hx))) + hx


def _log_sigmoid(x):
    return jnp.minimum(x, 0.0) - jnp.log1p(jnp.exp(-jnp.abs(x)))


def _dot(a, b):
    return jnp.dot(a, b, preferred_element_type=F32)


def _whole(memory_space=pltpu.VMEM):
    return pl.BlockSpec(memory_space=memory_space)


def _lane_chunk(c):
    return slice(c * LANES, (c + 1) * LANES)


def _subtiles(ref):
    bb, tt = ref.shape[0], ref.shape[1]
    return [(b, slice(s * SUB_T, (s + 1) * SUB_T)) for b in range(bb) for s in range(tt // SUB_T)]


def _token_block(B, T, rows, per_seq_m):
    tt = min(T, rows)
    bb = 1 if per_seq_m else min(B, rows // tt)
    return bb, tt


def _mod_kernel(cond_ref, w_ref, b_ref, o_ref):
    c = cond_ref[...]
    s = (c * _sigmoid(c)).astype(BF16)
    o_ref[...] = _dot(s, w_ref[...].astype(BF16)) + b_ref[...]


def _modulation(cond, w_mod, b_mod):
    rows = cond.shape[0]
    n = w_mod.shape[1]
    bn = D
    return pl.pallas_call(
        _mod_kernel,
        grid=(n // bn,),
        in_specs=[
            pl.BlockSpec((rows, D), lambda j: (0, 0)),
            pl.BlockSpec((D, bn), lambda j: (0, j)),
            pl.BlockSpec((1, bn), lambda j: (0, j)),
        ],
        out_specs=pl.BlockSpec((rows, bn), lambda j: (0, j)),
        out_shape=jax.ShapeDtypeStruct((rows, n), F32),
        compiler_params=pltpu.CompilerParams(dimension_semantics=("arbitrary",)),
        name="modulation",
    )(cond, w_mod, b_mod.reshape(1, n))


def _stage1_kernel(has_pe, *refs):
    if has_pe:
        x_ref, per_ref, pec_ref, refs = refs[0], refs[1], refs[2], refs[3:]
    else:
        x_ref, refs = refs[0], refs[1:]
    (m_ref, n1_ref, n2_ref, gv_ref, wg_ref, wu_ref, wd_ref, win_ref,
     x1_ref, zx_ref, za_ref) = refs
    m = m_ref[0]
    subs = _subtiles(x_ref)

    xs, hs = [], []
    for b, rs in subs:
        x = x_ref[b, rs, :]
        if has_pe:
            reps = GRID_W // SUBLANES
            r0 = rs.start // GRID_W
            row_half = jnp.concatenate(
                [jnp.tile(per_ref[r * SUBLANES:(r + 1) * SUBLANES, :], (reps, 1))
                 for r in range(r0, r0 + SUB_T // GRID_W)], axis=0)
            col_half = jnp.tile(pec_ref[...], (SUB_T // GRID_W, 1))
            x = x + jnp.concatenate([row_half, col_half], axis=1)
        xs.append(x)
        hs.append(_modulate(x, n1_ref[...], m[0:1], m[1:2]).astype(BF16))

    h2s = []
    for (b, rs), x, h in zip(subs, xs, hs):
        g = _dot(h, wg_ref[...])
        u = _dot(h, wu_ref[...])
        act = (_silu_of_half(g) * u).astype(BF16)
        x1 = x + (0.5 * m[2:3]) * _dot(act, wd_ref[:, 0:D])
        x1_ref[b, rs, :] = x1
        h2s.append(_modulate(x1, n2_ref[...], m[3:4], m[4:5]).astype(BF16))

    branches = ((2, _gelu_of_half), (3, lambda v: _rms(_gelu_of_half(v), gv_ref[...])),
                (4, _sigmoid_of_half), (5, _sigmoid_of_half), (1, _gelu_of_half))
    for (b, rs), h2 in zip(subs, h2s):
        for j, (src, fn) in enumerate(branches):
            z = _dot(h2, win_ref[:, src * D:(src + 1) * D])
            za_ref[b, rs, j * D:(j + 1) * D] = fn(z).astype(BF16)
        zx_ref[b, rs, :] = _dot(h2, win_ref[:, 0:D])


def _stage1(x, pe_tables, m, n1, n2, gv, wg, wu, wd, win):
    B, T, _ = x.shape
    has_pe = pe_tables is not None
    per_batch_m = m.shape[0] > 1
    bb, tt = _token_block(B, T, STAGE1_T, per_batch_m)
    m_map = (lambda t, b: (b, 0, 0)) if per_batch_m else (lambda t, b: (0, 0, 0))
    in_specs = [pl.BlockSpec((bb, tt, D), lambda t, b: (b, t, 0))]
    args = [x]
    if has_pe:
        pe_rows, pe_cols = pe_tables
        in_specs += [pl.BlockSpec((tt // GRID_W * SUBLANES, D // 2), lambda t, b: (t, 0)), _whole()]
        args += [pe_rows, pe_cols]
    in_specs += [pl.BlockSpec((1, N_MOD, D), m_map)] + [_whole()] * 7
    args += [m, n1, n2, gv, wg, wu, wd, win]
    blk = lambda w: pl.BlockSpec((bb, tt, w), lambda t, b: (b, t, 0))
    return pl.pallas_call(
        functools.partial(_stage1_kernel, has_pe),
        grid=(T // tt, B // bb),
        in_specs=in_specs,
        out_specs=[blk(D), blk(D), blk(Z_ACT)],
        out_shape=[jax.ShapeDtypeStruct((B, T, D), F32),
                   jax.ShapeDtypeStruct((B, T, D), F32),
                   jax.ShapeDtypeStruct((B, T, Z_ACT), BF16)],
        compiler_params=pltpu.CompilerParams(
            dimension_semantics=("arbitrary", "arbitrary"),
            vmem_limit_bytes=VMEM_LIMIT),
        name="stage1_ffn_inproj",
    )(*args)


def _gates_to_scratch(blk, xr, wg_ref, bg_ref, lam_ref, a_scr, b_scr):
    lo = blk * GATE_BLOCK
    c4 = (0.5 * LRU_C) * _log_sigmoid(lam_ref[:, lo:lo + GATE_BLOCK])
    t = jnp.tanh(_dot(xr.astype(BF16), wg_ref[blk]) + bg_ref[blk])
    t_r = t[:, 0:GATE_BLOCK]
    t_i = t[:, GATE_BLOCK:2 * GATE_BLOCK]
    log_a = t_r * c4 + c4
    a = jnp.exp(log_a)
    q = (-0.25 * jnp.tanh(log_a)) * (a * a + 1.0)
    half_mult = jnp.where(q > 0.0, q * lax.rsqrt(q), 0.0)
    w = half_mult * xr.astype(F32)
    bx = w * t_i + w
    for half in range(GATE_HALVES):
        c = blk * GATE_HALVES + half
        a_scr[c] = a[:, _lane_chunk(half)]
        b_scr[c] = bx[:, _lane_chunk(half)]


def _recurrence(reverse, tt, first, h0_ref, emit, a_scr, b_scr, carry_scr):
    nb = SCAN_BATCH

    @pl.when(first)
    def _():
        for c in range(N_LANE_CHUNKS):
            carry_scr[c] = h0_ref[:, _lane_chunk(c)]

    n_blocks = tt // SCAN_UNROLL
    order = range(SCAN_UNROLL - 1, -1, -1) if reverse else range(SCAN_UNROLL)

    def block(jb, hs):
        blk = (n_blocks - 1 - jb) if reverse else jb
        base = pl.multiple_of(blk * (SCAN_UNROLL * nb), SCAN_UNROLL * nb)
        hs = list(hs)
        for t in order:
            rows = pl.ds(base + t * nb, nb)
            for c in range(N_LANE_CHUNKS):
                hs[c] = a_scr[c, rows, :] * hs[c] + b_scr[c, rows, :]
                emit(rows, c, hs[c])
        return tuple(hs)

    hs = lax.fori_loop(0, n_blocks, block,
                       tuple(carry_scr[c] for c in range(N_LANE_CHUNKS)))
    for c in range(N_LANE_CHUNKS):
        carry_scr[c] = hs[c]
    return hs


def _scan_bwd_kernel(zc_ref, zl_ref, zrt_ref, h0_ref, cw_ref, cb_ref, wg_ref, bg_ref, lam_ref,
                     h_ref, xr_ref, ext_scr, a_scr, b_scr, carry_scr):
    nb, tt, _ = zc_ref.shape
    j = pl.program_id(1)
    nt = pl.num_programs(1)
    ti = nt - 1 - j
    rows = tt * nb
    lead = CONV_LEFT * nb

    for c in range(N_LANE_CHUNKS):
        cols = _lane_chunk(c)
        for k in range(CONV_LEFT):
            ext_scr[c, k * nb:(k + 1) * nb, :] = jnp.where(
                ti > 0, zl_ref[:, SUBLANES - CONV_LEFT + k, cols], 0.0)
        for k in range(CONV_RIGHT):
            ext_scr[c, lead + rows + k * nb:lead + rows + (k + 1) * nb, :] = jnp.where(
                ti < nt - 1, zrt_ref[:, k, cols], 0.0)
        for s in range(nb):
            ext_scr[c, pl.ds(lead + s, tt, stride=nb), :] = zc_ref[s, :, cols]

    for blk in range(N_GATE_BLOCKS):
        parts = []
        for half in range(GATE_HALVES):
            c = blk * GATE_HALVES + half
            cols = _lane_chunk(c)
            acc = ext_scr[c, 0:rows, :] * cw_ref[0:1, cols] + cb_ref[:, cols]
            for k in range(1, CONV_W):
                acc = acc + ext_scr[c, k * nb:k * nb + rows, :] * cw_ref[k:k + 1, cols]
            parts.append(acc)
        xr = jnp.concatenate(parts, axis=1)
        xr_ref[0, :, blk * GATE_BLOCK:(blk + 1) * GATE_BLOCK] = xr.astype(xr_ref.dtype)
        _gates_to_scratch(blk, xr, wg_ref, bg_ref, lam_ref, a_scr, b_scr)

    def emit(rows, c, h):
        h_ref[0, rows, _lane_chunk(c)] = h
    _recurrence(True, tt, j == 0, h0_ref, emit, a_scr, b_scr, carry_scr)


def _scan_fwd_kernel(xr_ref, h0_ref, hb_ref, wg_ref, bg_ref, lam_ref,
                     hsum_ref, hlast_ref, a_scr, b_scr, o_scr, carry_scr):
    nb, tt, _ = hsum_ref.shape
    for blk in range(N_GATE_BLOCKS):
        xr = xr_ref[0, :, blk * GATE_BLOCK:(blk + 1) * GATE_BLOCK]
        _gates_to_scratch(blk, xr, wg_ref, bg_ref, lam_ref, a_scr, b_scr)

    def emit(rows, c, hf):
        o_scr[c, rows, :] = hf + hb_ref[0, rows, _lane_chunk(c)]
    hs = _recurrence(False, tt, pl.program_id(1) == 0, h0_ref, emit, a_scr, b_scr, carry_scr)
    for c in range(N_LANE_CHUNKS):
        hlast_ref[:, _lane_chunk(c)] = hs[c]
        for s in range(nb):
            hsum_ref[s, :, _lane_chunk(c)] = o_scr[c, pl.ds(s, tt, stride=nb), :].astype(hsum_ref.dtype)


def _scan_scratch(nb, tt, n_tiles):
    tile = pltpu.VMEM((N_LANE_CHUNKS, tt * nb, LANES), F32)
    return [tile] * n_tiles + [pltpu.VMEM((N_LANE_CHUNKS, nb, LANES), F32)]


def _scan_bwd(zx, h0, conv_w, conv_b, wgate, bgate, lam):
    B, T, _ = zx.shape
    nb = SCAN_BATCH
    tt = min(T, SCAN_T)
    nt = T // tt
    tpb = tt // SUBLANES
    nhb = T // SUBLANES
    ti = lambda j: nt - 1 - j
    return pl.pallas_call(
        _scan_bwd_kernel,
        grid=(B // nb, nt),
        in_specs=[
            pl.BlockSpec((nb, tt, D), lambda g, j: (g, ti(j), 0)),
            pl.BlockSpec((nb, SUBLANES, D),
                         lambda g, j: (g, jnp.maximum(ti(j) * tpb - 1, 0), 0)),
            pl.BlockSpec((nb, SUBLANES, D),
                         lambda g, j: (g, jnp.minimum((ti(j) + 1) * tpb, nhb - 1), 0)),
            pl.BlockSpec((nb, D), lambda g, j: (g, 0)),
            _whole(), _whole(), _whole(), _whole(), _whole(),
        ],
        out_specs=[pl.BlockSpec((1, tt * nb, D), lambda g, j: (g, ti(j), 0))] * 2,
        out_shape=[jax.ShapeDtypeStruct((B // nb, T * nb, D), F32),
                   jax.ShapeDtypeStruct((B // nb, T * nb, D), BF16)],
        scratch_shapes=[pltpu.VMEM((N_LANE_CHUNKS, (tt + CONV_W - 1) * nb, LANES), F32)]
        + _scan_scratch(nb, tt, 2),
        compiler_params=pltpu.CompilerParams(
            dimension_semantics=("arbitrary", "arbitrary"),
            vmem_limit_bytes=VMEM_LIMIT),
        name="scan_bwd",
    )(zx, zx, zx, h0, conv_w, conv_b, wgate, bgate, lam)


def _scan_fwd(xr_tm, hb_tm, h0, wgate, bgate, lam):
    G, rows_total, _ = xr_tm.shape
    nb = SCAN_BATCH
    T = rows_total // nb
    tt = min(T, SCAN_T)
    tm_tile = pl.BlockSpec((1, tt * nb, D), lambda g, j: (g, j, 0))
    return pl.pallas_call(
        _scan_fwd_kernel,
        grid=(G, T // tt),
        in_specs=[
            tm_tile,
            pl.BlockSpec((nb, D), lambda g, j: (g, 0)),
            tm_tile,
            _whole(), _whole(), _whole(),
        ],
        out_specs=[pl.BlockSpec((nb, tt, D), lambda g, j: (g, j, 0)),
                   pl.BlockSpec((nb, D), lambda g, j: (g, 0))],
        out_shape=[jax.ShapeDtypeStruct((G * nb, T, D), BF16),
                   jax.ShapeDtypeStruct((G * nb, D), F32)],
        scratch_shapes=_scan_scratch(nb, tt, 3),
        compiler_params=pltpu.CompilerParams(
            dimension_semantics=("arbitrary", "arbitrary"),
            vmem_limit_bytes=VMEM_LIMIT),
        name="scan_fwd",
    )(xr_tm, h0, hb_tm, wgate, bgate, lam)


def _stage3_kernel(x1_ref, hsum_ref, za_ref, m_ref, ws_ref, bs_ref,
                   wbr_ref, wbg_ref, wo_ref, n3_ref, wg_ref, wu_ref, wd_ref, nf_ref,
                   y_ref, mixed_scr):
    m = m_ref[0]
    subs = _subtiles(x1_ref)

    x2s, hs = [], []
    for i, (b, rs) in enumerate(subs):
        zcol = lambda c: za_ref[b, rs, c * D:(c + 1) * D]
        y_rnn = zcol(ZA_RNN_GATE).astype(F32) * hsum_ref[b, rs, :].astype(F32)
        rnn_proj = _dot(y_rnn.astype(BF16), wbr_ref[:, 0:D])
        n_chunks = SUB_T // CHUNK
        for g in range(GROUPS):
            cs = _lane_chunk(g)
            vn_g = jnp.concatenate(
                [za_ref[b, rs.start + n * CHUNK:rs.start + (n + 1) * CHUNK, D + g * LANES:D + (g + 1) * LANES]
                 for n in range(n_chunks)], axis=1)
            mix_g = _dot(ws_ref[g], vn_g)
            for n in range(n_chunks):
                mixed_scr[i, n * CHUNK:(n + 1) * CHUNK, cs] = mix_g[:, _lane_chunk(n)] + bs_ref[:, cs]
        y_g = zcol(0).astype(F32) * mixed_scr[i]
        g_proj = _dot(y_g.astype(BF16), wbg_ref[:, 0:D])
        merged = zcol(2).astype(F32) * rnn_proj + zcol(3).astype(F32) * g_proj
        x2 = x1_ref[b, rs, :] + m[5:6] * _dot(merged.astype(BF16), wo_ref[:, 0:D])
        x2s.append(x2)
        hs.append(_modulate(x2, n3_ref[...], m[6:7], m[7:8]).astype(BF16))

    for (b, rs), x2, h in zip(subs, x2s, hs):
        g = _dot(h, wg_ref[...])
        uu = _dot(h, wu_ref[...])
        act = (_silu_of_half(g) * uu).astype(BF16)
        x3 = x2 + (0.5 * m[8:9]) * _dot(act, wd_ref[:, 0:D])
        y_ref[b, rs, :] = _rms(x3, nf_ref[...])


def _stage3(x1, h_sum, za, m, ws, bs, wbr, wbg, wo, n3, wg, wu, wd, nf):
    B, T, _ = x1.shape
    per_batch_m = m.shape[0] > 1
    bb, tt = _token_block(B, T, STAGE3_T, per_batch_m)
    m_map = (lambda t, b: (b, 0, 0)) if per_batch_m else (lambda t, b: (0, 0, 0))
    blk = lambda w: pl.BlockSpec((bb, tt, w), lambda t, b: (b, t, 0))
    return pl.pallas_call(
        _stage3_kernel,
        grid=(T // tt, B // bb),
        in_specs=[blk(D), blk(D), blk(Z_ACT),
                  pl.BlockSpec((1, N_MOD, D), m_map)] + [_whole()] * 10,
        out_specs=blk(D),
        out_shape=jax.ShapeDtypeStruct((B, T, D), F32),
        scratch_shapes=[pltpu.VMEM((bb * tt // SUB_T, SUB_T, D), F32)],
        compiler_params=pltpu.CompilerParams(
            dimension_semantics=("arbitrary", "arbitrary"),
            vmem_limit_bytes=VMEM_LIMIT),
        name="stage3_mix_ffn",
    )(x1, h_sum, za, m, ws, bs, wbr, wbg, wo, n3, wg, wu, wd, nf)


def _grid_pos_tables(n_tokens):
    q = D // 4
    freqs = 1.0 / (10000.0 ** (jnp.arange(q, dtype=F32) / q))
    half = lambda idx: jnp.concatenate(
        [jnp.sin(idx[:, None] * freqs), jnp.cos(idx[:, None] * freqs)], axis=-1)
    rows = half(jnp.arange(n_tokens // GRID_W).astype(F32))
    cols = half(jnp.arange(GRID_W).astype(F32))
    return jnp.repeat(rows, SUBLANES, axis=0), cols


def _gate_weights(w_r, w_i, b_r, b_i):
    def bd(w):
        w4 = w.reshape(N_GATE_BLOCKS, HEADS_PER_BLOCK, HEAD_DIM, HEAD_DIM)
        eye = jnp.eye(HEADS_PER_BLOCK, dtype=w.dtype)
        return jnp.einsum('ghij,hk->ghikj', w4, eye).reshape(N_GATE_BLOCKS, GATE_BLOCK, GATE_BLOCK)
    wg = (0.5 * jnp.concatenate([bd(w_r), bd(w_i)], axis=-1)).astype(BF16)
    bg = 0.5 * jnp.concatenate([b_r.reshape(N_GATE_BLOCKS, 1, GATE_BLOCK),
                                b_i.reshape(N_GATE_BLOCKS, 1, GATE_BLOCK)], axis=-1)
    return wg, bg


def kernel(x_prompt, x_sample, state_rnn_fwd, state_rnn_bwd, c, c_ctx, w_mod, b_mod, norm1, norm2, norm3, ff1_gate, ff1_up, ff1_down, w_in, conv_w, conv_b, w_r, b_r, w_i, b_i, lam, gmlp_norm, w_s, b_s, w_br, w_bg, w_out, ff2_gate, ff2_up, ff2_down, norm_f):
    l = 0
    nbatch_lat = c.shape[0]
    cond_rows = 2 * SUBLANES
    cond = jnp.zeros((cond_rows, D), F32).at[:nbatch_lat].set(c).at[nbatch_lat].set(c_ctx)
    m_all = _modulation(cond, w_mod[l], b_mod[l])
    m_lat = m_all[:nbatch_lat].reshape(nbatch_lat, N_MOD, D)
    m_ctx = m_all[nbatch_lat:nbatch_lat + 1].reshape(1, N_MOD, D)

    row = lambda v: v.reshape(1, D)
    bf = lambda w: w.astype(BF16)

    def bfp(w):
        assert (w.shape[-1] // LANES) % 2 == 0
        return jnp.pad(w.astype(BF16), ((0, 0), (0, LANES)))
    n1, n2, n3, nf, gv = row(norm1[l]), row(norm2[l]), row(norm3[l]), row(norm_f), row(gmlp_norm[l])
    wg1, wu1, wd1 = bf(0.5 * ff1_gate[l]), bf(ff1_up[l]), bfp(ff1_down[l])
    wg2, wu2, wd2 = bf(0.5 * ff2_gate[l]), bf(ff2_up[l]), bfp(ff2_down[l])
    in_scale = jnp.concatenate([jnp.ones((D,), F32), jnp.full((N_BRANCH * D,), 0.5, F32)])
    win, wbr, wbg, wo = bfp(w_in[l] * in_scale), bfp(w_br[l]), bfp(w_bg[l]), bfp(w_out[l])
    ws = bf(w_s[l])
    bs = jnp.repeat(b_s[l].T, D // GROUPS, axis=1)
    cw, cb, lam_l = conv_w[l], row(conv_b[l]), lam[l]
    (wgt_f, bgt_f), (wgt_b, bgt_b) = [
        _gate_weights(w_r[l, d], w_i[l, d], b_r[l, d], b_i[l, d]) for d in range(2)]

    def trunk(x, pe_tables, m, h0_f, h0_b):
        x1, zx, za = _stage1(x, pe_tables, m, n1, n2, gv, wg1, wu1, wd1, win)
        hb_tm, xr_tm = _scan_bwd(zx, h0_b, cw, cb, wgt_b, bgt_b, row(lam_l[1]))
        h_sum, hf_last = _scan_fwd(xr_tm, hb_tm, h0_f, wgt_f, bgt_f, row(lam_l[0]))
        y = _stage3(x1, h_sum, za, m, ws, bs, wbr, wbg, wo, n3, wg2, wu2, wd2, nf)
        return y, hf_last, hb_tm[:, :SCAN_BATCH, :].reshape(-1, 1, D)

    zeros = jnp.zeros((x_prompt.shape[0], D), F32)
    y_prompt, hf_last, new_b = trunk(x_prompt, None, m_ctx, zeros, zeros)
    new_f = hf_last[:, None, :]

    y_sample, _, _ = trunk(x_sample, _grid_pos_tables(x_sample.shape[1]), m_lat,
                           state_rnn_fwd[:, l], state_rnn_bwd[:, l])
    return (y_prompt, y_sample, new_f, new_b)
```

```python
import functools

import jax
import jax.numpy as jnp
from jax import lax
from jax.experimental import pallas as pl
from jax.experimental.pallas import tpu as pltpu

D = 1024
D_FF = 2816
N_MOD = 9
EPS = 1e-6
CHUNK = 128
GROUPS = 8
HEADS = 16
HEAD_DIM = 64
CONV_W = 4
CONV_LEFT = 2
CONV_RIGHT = CONV_W - 1 - CONV_LEFT
LRU_C = 8.0
GRID_W = 64
N_BRANCH = 5
Z_ACT = N_BRANCH * D
ZA_RNN_GATE = 4

LANES = 128
SUBLANES = 8
N_LANE_CHUNKS = D // LANES
HEADS_PER_BLOCK = 4
GATE_BLOCK = HEADS_PER_BLOCK * HEAD_DIM
N_GATE_BLOCKS = D // GATE_BLOCK
GATE_HALVES = GATE_BLOCK // LANES

SUB_T = 256
STAGE1_T = 512
STAGE3_T = 512
SCAN_BATCH = SUBLANES
SCAN_T = 128
SCAN_UNROLL = 8
VMEM_LIMIT = 61 * 1024 * 1024

F32 = jnp.float32
BF16 = jnp.bfloat16


def _rms(x, g):
    ms = jnp.mean(x * x, axis=-1, keepdims=True)
    return (x * lax.rsqrt(ms + EPS)) * g


def _modulate(x, g, shift, scale):
    return _rms(x, g * (1.0 + scale)) + shift


def _sigmoid(x):
    return 0.5 * jnp.tanh(0.5 * x) + 0.5


def _sigmoid_of_half(hx):
    return 0.5 * jnp.tanh(hx) + 0.5


def _silu_of_half(hx):
    return hx * jnp.tanh(hx) + hx


GELU_K1 = 2.0 * 0.7978845608028654
GELU_K2 = 4.0 * GELU_K1 * 0.044715


def _gelu_of_half(hx):
    return hx * jnp.tanh(hx * (GELU_K1 + GELU_K2 * (hx * hx))) + hx


def _log_sigmoid(x):
    return jnp.minimum(x, 0.0) - jnp.log1p(jnp.exp(-jnp.abs(x)))


def _dot(a, b):
    return jnp.dot(a, b, preferred_element_type=F32)


def _whole(memory_space=pltpu.VMEM):
    return pl.BlockSpec(memory_space=memory_space)


def _lane_chunk(c):
    return slice(c * LANES, (c + 1) * LANES)


def _subtiles(ref):
    bb, tt = ref.shape[0], ref.shape[1]
    return [(b, slice(s * SUB_T, (s + 1) * SUB_T)) for b in range(bb) for s in range(tt // SUB_T)]


def _token_block(B, T, rows, per_seq_m):
    tt = min(T, rows)
    bb = 1 if per_seq_m else min(B, rows // tt)
    return bb, tt


def _mod_kernel(cond_ref, w_ref, b_ref, o_ref):
    c = cond_ref[...]
    s = (c * _sigmoid(c)).astype(BF16)
    o_ref[...] = _dot(s, w_ref[...].astype(BF16)) + b_ref[...]


def _modulation(cond, w_mod, b_mod):
    rows = cond.shape[0]
    n = w_mod.shape[1]
    bn = D
    return pl.pallas_call(
        _mod_kernel,
        grid=(n // bn,),
        in_specs=[
            pl.BlockSpec((rows, D), lambda j: (0, 0)),
            pl.BlockSpec((D, bn), lambda j: (0, j)),
            pl.BlockSpec((1, bn), lambda j: (0, j)),
        ],
        out_specs=pl.BlockSpec((rows, bn), lambda j: (0, j)),
        out_shape=jax.ShapeDtypeStruct((rows, n), F32),
        compiler_params=pltpu.CompilerParams(dimension_semantics=("arbitrary",)),
        name="modulation",
    )(cond, w_mod, b_mod.reshape(1, n))


def _stage1_kernel(has_pe, *refs):
    if has_pe:
        x_ref, per_ref, pec_ref, refs = refs[0], refs[1], refs[2], refs[3:]
    else:
        x_ref, refs = refs[0], refs[1:]
    (m_ref, n1_ref, n2_ref, gv_ref, wg_ref, wu_ref, wd_ref, win_ref,
     x1_ref, zx_ref, za_ref) = refs
    m = m_ref[0]
    subs = _subtiles(x_ref)

    xs, hs = [], []
    for b, rs in subs:
        x = x_ref[b, rs, :]
        if has_pe:
            reps = GRID_W // SUBLANES
            r0 = rs.start // GRID_W
            row_half = jnp.concatenate(
                [jnp.tile(per_ref[r * SUBLANES:(r + 1) * SUBLANES, :], (reps, 1))
                 for r in range(r0, r0 + SUB_T // GRID_W)], axis=0)
            col_half = jnp.tile(pec_ref[...], (SUB_T // GRID_W, 1))
            x = x + jnp.concatenate([row_half, col_half], axis=1)
        xs.append(x)
        hs.append(_modulate(x, n1_ref[...], m[0:1], m[1:2]).astype(BF16))

    h2s = []
    for (b, rs), x, h in zip(subs, xs, hs):
        g = _dot(h, wg_ref[...])
        u = _dot(h, wu_ref[...])
        act = (_silu_of_half(g) * u).astype(BF16)
        x1 = x + (0.5 * m[2:3]) * _dot(act, wd_ref[:, 0:D])
        x1_ref[b, rs, :] = x1
        h2s.append(_modulate(x1, n2_ref[...], m[3:4], m[4:5]).astype(BF16))

    branches = ((2, _gelu_of_half), (3, lambda v: _rms(_gelu_of_half(v), gv_ref[...])),
                (4, _sigmoid_of_half), (5, _sigmoid_of_half), (1, _gelu_of_half))
    for (b, rs), h2 in zip(subs, h2s):
        for j, (src, fn) in enumerate(branches):
            z = _dot(h2, win_ref[:, src * D:(src + 1) * D])
            za_ref[b, rs, j * D:(j + 1) * D] = fn(z).astype(BF16)
        zx_ref[b, rs, :] = _dot(h2, win_ref[:, 0:D])


def _stage1(x, pe_tables, m, n1, n2, gv, wg, wu, wd, win):
    B, T, _ = x.shape
    has_pe = pe_tables is not None
    per_batch_m = m.shape[0] > 1
    bb, tt = _token_block(B, T, STAGE1_T, per_batch_m)
    m_map = (lambda t, b: (b, 0, 0)) if per_batch_m else (lambda t, b: (0, 0, 0))
    in_specs = [pl.BlockSpec((bb, tt, D), lambda t, b: (b, t, 0))]
    args = [x]
    if has_pe:
        pe_rows, pe_cols = pe_tables
        in_specs += [pl.BlockSpec((tt // GRID_W * SUBLANES, D // 2), lambda t, b: (t, 0)), _whole()]
        args += [pe_rows, pe_cols]
    in_specs += [pl.BlockSpec((1, N_MOD, D), m_map)] + [_whole()] * 7
    args += [m, n1, n2, gv, wg, wu, wd, win]
    blk = lambda w: pl.BlockSpec((bb, tt, w), lambda t, b: (b, t, 0))
    return pl.pallas_call(
        functools.partial(_stage1_kernel, has_pe),
        grid=(T // tt, B // bb),
        in_specs=in_specs,
        out_specs=[blk(D), blk(D), blk(Z_ACT)],
        out_shape=[jax.ShapeDtypeStruct((B, T, D), F32),
                   jax.ShapeDtypeStruct((B, T, D), F32),
                   jax.ShapeDtypeStruct((B, T, Z_ACT), BF16)],
        compiler_params=pltpu.CompilerParams(
            dimension_semantics=("arbitrary", "arbitrary"),
            vmem_limit_bytes=VMEM_LIMIT),
        name="stage1_ffn_inproj",
    )(*args)


def _gates(blk, xr, wg_ref, bg_ref, lam_ref):
    lo = blk * GATE_BLOCK
    c4 = (0.5 * LRU_C) * _log_sigmoid(lam_ref[:, lo:lo + GATE_BLOCK])
    t = jnp.tanh(_dot(xr.astype(BF16), wg_ref[blk]) + bg_ref[blk])
    t_r = t[:, 0:GATE_BLOCK]
    t_i = t[:, GATE_BLOCK:2 * GATE_BLOCK]
    log_a = t_r * c4 + c4
    a = jnp.exp(log_a)
    q = (-0.25 * jnp.tanh(log_a)) * (a * a + 1.0)
    half_mult = jnp.where(q > 0.0, q * lax.rsqrt(q), 0.0)
    w = half_mult * xr.astype(F32)
    return a, w * t_i + w


def _init_carry(first, h0_ref, carry_scr):
    @pl.when(first)
    def _():
        for c in range(N_LANE_CHUNKS):
            carry_scr[c] = h0_ref[:, _lane_chunk(c)]


def _recurrence_block(blk, reverse, tt, a, bx, emit, carry_scr):
    nb = SCAN_BATCH
    for half in range(GATE_HALVES):
        c = blk * GATE_HALVES + half
        cs = _lane_chunk(half)
        h = carry_scr[c]
        for t in (range(tt - 1, -1, -1) if reverse else range(tt)):
            rs = slice(t * nb, (t + 1) * nb)
            h = a[rs, cs] * h + bx[rs, cs]
            emit(rs, c, h)
        carry_scr[c] = h


def _scan_bwd_kernel(zc_ref, zl_ref, zrt_ref, h0_ref, cw_ref, cb_ref, wg_ref, bg_ref, lam_ref,
                     h_ref, xr_ref, ext_scr, carry_scr):
    nb, tt, _ = zc_ref.shape
    j = pl.program_id(1)
    nt = pl.num_programs(1)
    ti = nt - 1 - j
    rows = tt * nb
    lead = CONV_LEFT * nb

    _init_carry(j == 0, h0_ref, carry_scr)

    def emit(rows, c, h):
        h_ref[0, rows, _lane_chunk(c)] = h

    for c in range(N_LANE_CHUNKS):
        cols = _lane_chunk(c)
        for k in range(CONV_LEFT):
            ext_scr[c, k * nb:(k + 1) * nb, :] = jnp.where(
                ti > 0, zl_ref[:, SUBLANES - CONV_LEFT + k, cols], 0.0)
        for k in range(CONV_RIGHT):
            ext_scr[c, lead + rows + k * nb:lead + rows + (k + 1) * nb, :] = jnp.where(
                ti < nt - 1, zrt_ref[:, k, cols], 0.0)
        for s in range(nb):
            ext_scr[c, pl.ds(lead + s, tt, stride=nb), :] = zc_ref[s, :, cols]

    for blk in range(N_GATE_BLOCKS):
        parts = []
        for half in range(GATE_HALVES):
            c = blk * GATE_HALVES + half
            cols = _lane_chunk(c)
            acc = ext_scr[c, 0:rows, :] * cw_ref[0:1, cols] + cb_ref[:, cols]
            for k in range(1, CONV_W):
                acc = acc + ext_scr[c, k * nb:k * nb + rows, :] * cw_ref[k:k + 1, cols]
            parts.append(acc)
        xr = jnp.concatenate(parts, axis=1)
        xr_ref[0, :, blk * GATE_BLOCK:(blk + 1) * GATE_BLOCK] = xr.astype(xr_ref.dtype)
        a, bx = _gates(blk, xr, wg_ref, bg_ref, lam_ref)
        _recurrence_block(blk, True, tt, a, bx, emit, carry_scr)


def _scan_fwd_kernel(xr_ref, h0_ref, hb_ref, wg_ref, bg_ref, lam_ref,
                     hsum_ref, hlast_ref, o_scr, carry_scr):
    nb, tt, _ = hsum_ref.shape
    _init_carry(pl.program_id(1) == 0, h0_ref, carry_scr)

    def emit(rows, c, hf):
        o_scr[c, rows, :] = hf + hb_ref[0, rows, _lane_chunk(c)]

    for blk in range(N_GATE_BLOCKS):
        xr = xr_ref[0, :, blk * GATE_BLOCK:(blk + 1) * GATE_BLOCK]
        a, bx = _gates(blk, xr, wg_ref, bg_ref, lam_ref)
        _recurrence_block(blk, False, tt, a, bx, emit, carry_scr)
    for c in range(N_LANE_CHUNKS):
        hlast_ref[:, _lane_chunk(c)] = carry_scr[c]
        for s in range(nb):
            hsum_ref[s, :, _lane_chunk(c)] = o_scr[c, pl.ds(s, tt, stride=nb), :].astype(hsum_ref.dtype)


def _scan_scratch(nb, tt, n_tiles):
    tile = pltpu.VMEM((N_LANE_CHUNKS, tt * nb, LANES), F32)
    return [tile] * n_tiles + [pltpu.VMEM((N_LANE_CHUNKS, nb, LANES), F32)]


def _scan_bwd(zx, h0, conv_w, conv_b, wgate, bgate, lam):
    B, T, _ = zx.shape
    nb = SCAN_BATCH
    tt = min(T, SCAN_T)
    nt = T // tt
    tpb = tt // SUBLANES
    nhb = T // SUBLANES
    ti = lambda j: nt - 1 - j
    return pl.pallas_call(
        _scan_bwd_kernel,
        grid=(B // nb, nt),
        in_specs=[
            pl.BlockSpec((nb, tt, D), lambda g, j: (g, ti(j), 0)),
            pl.BlockSpec((nb, SUBLANES, D),
                         lambda g, j: (g, jnp.maximum(ti(j) * tpb - 1, 0), 0)),
            pl.BlockSpec((nb, SUBLANES, D),
                         lambda g, j: (g, jnp.minimum((ti(j) + 1) * tpb, nhb - 1), 0)),
            pl.BlockSpec((nb, D), lambda g, j: (g, 0)),
            _whole(), _whole(), _whole(), _whole(), _whole(),
        ],
        out_specs=[pl.BlockSpec((1, tt * nb, D), lambda g, j: (g, ti(j), 0))] * 2,
        out_shape=[jax.ShapeDtypeStruct((B // nb, T * nb, D), F32),
                   jax.ShapeDtypeStruct((B // nb, T * nb, D), BF16)],
        scratch_shapes=[pltpu.VMEM((N_LANE_CHUNKS, (tt + CONV_W - 1) * nb, LANES), F32)]
        + _scan_scratch(nb, tt, 0),
        compiler_params=pltpu.CompilerParams(
            dimension_semantics=("arbitrary", "arbitrary"),
            vmem_limit_bytes=VMEM_LIMIT),
        name="scan_bwd",
    )(zx, zx, zx, h0, conv_w, conv_b, wgate, bgate, lam)


def _scan_fwd(xr_tm, hb_tm, h0, wgate, bgate, lam):
    G, rows_total, _ = xr_tm.shape
    nb = SCAN_BATCH
    T = rows_total // nb
    tt = min(T, SCAN_T)
    tm_tile = pl.BlockSpec((1, tt * nb, D), lambda g, j: (g, j, 0))
    return pl.pallas_call(
        _scan_fwd_kernel,
        grid=(G, T // tt),
        in_specs=[
            tm_tile,
            pl.BlockSpec((nb, D), lambda g, j: (g, 0)),
            tm_tile,
            _whole(), _whole(), _whole(),
        ],
        out_specs=[pl.BlockSpec((nb, tt, D), lambda g, j: (g, j, 0)),
                   pl.BlockSpec((nb, D), lambda g, j: (g, 0))],
        out_shape=[jax.ShapeDtypeStruct((G * nb, T, D), BF16),
                   jax.ShapeDtypeStruct((G * nb, D), F32)],
        scratch_shapes=_scan_scratch(nb, tt, 1),
        compiler_params=pltpu.CompilerParams(
            dimension_semantics=("arbitrary", "arbitrary"),
            vmem_limit_bytes=VMEM_LIMIT),
        name="scan_fwd",
    )(xr_tm, h0, hb_tm, wgate, bgate, lam)


def _stage3_kernel(x1_ref, hsum_ref, za_ref, m_ref, ws_ref, bs_ref,
                   wbr_ref, wbg_ref, wo_ref, n3_ref, wg_ref, wu_ref, wd_ref, nf_ref,
                   y_ref, mixed_scr):
    m = m_ref[0]
    subs = _subtiles(x1_ref)

    x2s, hs = [], []
    for i, (b, rs) in enumerate(subs):
        zcol = lambda c: za_ref[b, rs, c * D:(c + 1) * D]
        y_rnn = zcol(ZA_RNN_GATE).astype(F32) * hsum_ref[b, rs, :].astype(F32)
        rnn_proj = _dot(y_rnn.astype(BF16), wbr_ref[:, 0:D])
        n_chunks = SUB_T // CHUNK
        for g in range(GROUPS):
            cs = _lane_chunk(g)
            vn_g = jnp.concatenate(
                [za_ref[b, rs.start + n * CHUNK:rs.start + (n + 1) * CHUNK, D + g * LANES:D + (g + 1) * LANES]
                 for n in range(n_chunks)], axis=1)
            mix_g = _dot(ws_ref[g], vn_g)
            for n in range(n_chunks):
                mixed_scr[i, n * CHUNK:(n + 1) * CHUNK, cs] = mix_g[:, _lane_chunk(n)] + bs_ref[:, cs]
        y_g = zcol(0).astype(F32) * mixed_scr[i]
        g_proj = _dot(y_g.astype(BF16), wbg_ref[:, 0:D])
        merged = zcol(2).astype(F32) * rnn_proj + zcol(3).astype(F32) * g_proj
        x2 = x1_ref[b, rs, :] + m[5:6] * _dot(merged.astype(BF16), wo_ref[:, 0:D])
        x2s.append(x2)
        hs.append(_modulate(x2, n3_ref[...], m[6:7], m[7:8]).astype(BF16))

    for (b, rs), x2, h in zip(subs, x2s, hs):
        g = _dot(h, wg_ref[...])
        uu = _dot(h, wu_ref[...])
        act = (_silu_of_half(g) * uu).astype(BF16)
        x3 = x2 + (0.5 * m[8:9]) * _dot(act, wd_ref[:, 0:D])
        y_ref[b, rs, :] = _rms(x3, nf_ref[...])


def _stage3(x1, h_sum, za, m, ws, bs, wbr, wbg, wo, n3, wg, wu, wd, nf):
    B, T, _ = x1.shape
    per_batch_m = m.shape[0] > 1
    bb, tt = _token_block(B, T, STAGE3_T, per_batch_m)
    m_map = (lambda t, b: (b, 0, 0)) if per_batch_m else (lambda t, b: (0, 0, 0))
    blk = lambda w: pl.BlockSpec((bb, tt, w), lambda t, b: (b, t, 0))
    return pl.pallas_call(
        _stage3_kernel,
        grid=(T // tt, B // bb),
        in_specs=[blk(D), blk(D), blk(Z_ACT),
                  pl.BlockSpec((1, N_MOD, D), m_map)] + [_whole()] * 10,
        out_specs=blk(D),
        out_shape=jax.ShapeDtypeStruct((B, T, D), F32),
        scratch_shapes=[pltpu.VMEM((bb * tt // SUB_T, SUB_T, D), F32)],
        compiler_params=pltpu.CompilerParams(
            dimension_semantics=("arbitrary", "arbitrary"),
            vmem_limit_bytes=VMEM_LIMIT),
        name="stage3_mix_ffn",
    )(x1, h_sum, za, m, ws, bs, wbr, wbg, wo, n3, wg, wu, wd, nf)


def _grid_pos_tables(n_tokens):
    q = D // 4
    freqs = 1.0 / (10000.0 ** (jnp.arange(q, dtype=F32) / q))
    half = lambda idx: jnp.concatenate(
        [jnp.sin(idx[:, None] * freqs), jnp.cos(idx[:, None] * freqs)], axis=-1)
    rows = half(jnp.arange(n_tokens // GRID_W).astype(F32))
    cols = half(jnp.arange(GRID_W).astype(F32))
    return jnp.repeat(rows, SUBLANES, axis=0), cols


def _gate_weights(w_r, w_i, b_r, b_i):
    def bd(w):
        w4 = w.reshape(N_GATE_BLOCKS, HEADS_PER_BLOCK, HEAD_DIM, HEAD_DIM)
        eye = jnp.eye(HEADS_PER_BLOCK, dtype=w.dtype)
        return jnp.einsum('ghij,hk->ghikj', w4, eye).reshape(N_GATE_BLOCKS, GATE_BLOCK, GATE_BLOCK)
    wg = (0.5 * jnp.concatenate([bd(w_r), bd(w_i)], axis=-1)).astype(BF16)
    bg = 0.5 * jnp.concatenate([b_r.reshape(N_GATE_BLOCKS, 1, GATE_BLOCK),
                                b_i.reshape(N_GATE_BLOCKS, 1, GATE_BLOCK)], axis=-1)
    return wg, bg


def kernel(x_prompt, x_sample, state_rnn_fwd, state_rnn_bwd, c, c_ctx, w_mod, b_mod, norm1, norm2, norm3, ff1_gate, ff1_up, ff1_down, w_in, conv_w, conv_b, w_r, b_r, w_i, b_i, lam, gmlp_norm, w_s, b_s, w_br, w_bg, w_out, ff2_gate, ff2_up, ff2_down, norm_f):
    l = 0
    nbatch_lat = c.shape[0]
    cond_rows = 2 * SUBLANES
    cond = jnp.zeros((cond_rows, D), F32).at[:nbatch_lat].set(c).at[nbatch_lat].set(c_ctx)
    m_all = _modulation(cond, w_mod[l], b_mod[l])
    m_lat = m_all[:nbatch_lat].reshape(nbatch_lat, N_MOD, D)
    m_ctx = m_all[nbatch_lat:nbatch_lat + 1].reshape(1, N_MOD, D)

    row = lambda v: v.reshape(1, D)
    bf = lambda w: w.astype(BF16)

    def bfp(w):
        assert (w.shape[-1] // LANES) % 2 == 0
        return jnp.pad(w.astype(BF16), ((0, 0), (0, LANES)))
    n1, n2, n3, nf, gv = row(norm1[l]), row(norm2[l]), row(norm3[l]), row(norm_f), row(gmlp_norm[l])
    wg1, wu1, wd1 = bf(0.5 * ff1_gate[l]), bf(ff1_up[l]), bfp(ff1_down[l])
    wg2, wu2, wd2 = bf(0.5 * ff2_gate[l]), bf(ff2_up[l]), bfp(ff2_down[l])
    in_scale = jnp.concatenate([jnp.ones((D,), F32), jnp.full((N_BRANCH * D,), 0.5, F32)])
    win, wbr, wbg, wo = bfp(w_in[l] * in_scale), bfp(w_br[l]), bfp(w_bg[l]), bfp(w_out[l])
    ws = bf(w_s[l])
    bs = jnp.repeat(b_s[l].T, D // GROUPS, axis=1)
    cw, cb, lam_l = conv_w[l], row(conv_b[l]), lam[l]
    (wgt_f, bgt_f), (wgt_b, bgt_b) = [
        _gate_weights(w_r[l, d], w_i[l, d], b_r[l, d], b_i[l, d]) for d in range(2)]

    def trunk(x, pe_tables, m, h0_f, h0_b):
        x1, zx, za = _stage1(x, pe_tables, m, n1, n2, gv, wg1, wu1, wd1, win)
        hb_tm, xr_tm = _scan_bwd(zx, h0_b, cw, cb, wgt_b, bgt_b, row(lam_l[1]))
        h_sum, hf_last = _scan_fwd(xr_tm, hb_tm, h0_f, wgt_f, bgt_f, row(lam_l[0]))
        y = _stage3(x1, h_sum, za, m, ws, bs, wbr, wbg, wo, n3, wg2, wu2, wd2, nf)
        return y, hf_last, hb_tm[:, :SCAN_BATCH, :].reshape(-1, 1, D)

    zeros = jnp.zeros((x_prompt.shape[0], D), F32)
    y_prompt, hf_last, new_b = trunk(x_prompt, None, m_ctx, zeros, zeros)
    new_f = hf_last[:, None, :]

    y_sample, _, _ = trunk(x_sample, _grid_pos_tables(x_sample.shape[1]), m_lat,
                           state_rnn_fwd[:, l], state_rnn_bwd[:, l])
    return (y_prompt, y_sample, new_f, new_b)
```

```python
import functools

import jax
import jax.numpy as jnp
from jax import lax
from jax.experimental import pallas as pl
from jax.experimental.pallas import tpu as pltpu

D = 1024
D_FF = 2816
N_MOD = 9
EPS = 1e-6
CHUNK = 128
GROUPS = 8
HEADS = 16
HEAD_DIM = 64
CONV_W = 4
CONV_LEFT = 2
CONV_RIGHT = CONV_W - 1 - CONV_LEFT
LRU_C = 8.0
GRID_W = 64
N_BRANCH = 5
Z_ACT = N_BRANCH * D
ZA_RNN_GATE = 4

LANES = 128
SUBLANES = 8
N_LANE_CHUNKS = D // LANES
HEADS_PER_BLOCK = 4
GATE_BLOCK = HEADS_PER_BLOCK * HEAD_DIM
N_GATE_BLOCKS = D // GATE_BLOCK
GATE_HALVES = GATE_BLOCK // LANES

SUB_T = 256
STAGE1_T = 512
STAGE3_T = 512
SCAN_BATCH = SUBLANES
SCAN_T = 256
SCAN_UNROLL = 8
VMEM_LIMIT = 61 * 1024 * 1024

F32 = jnp.float32
BF16 = jnp.bfloat16


def _rms(x, g):
    ms = jnp.mean(x * x, axis=-1, keepdims=True)
    return (x * lax.rsqrt(ms + EPS)) * g


def _modulate(x, g, shift, scale):
    return _rms(x, g * (1.0 + scale)) + shift


def _sigmoid(x):
    return 0.5 * jnp.tanh(0.5 * x) + 0.5


def _sigmoid_of_half(hx):
    return 0.5 * jnp.tanh(hx) + 0.5


def _silu_of_half(hx):
    return hx * jnp.tanh(hx) + hx


GELU_K1 = 2.0 * 0.7978845608028654
GELU_K2 = 4.0 * GELU_K1 * 0.044715


def _gelu_of_half(hx):
    return hx * jnp.tanh(hx * (GELU_K1 + GELU_K2 * (hx * hx))) + hx


def _log_sigmoid(x):
    return jnp.minimum(x, 0.0) - jnp.log1p(jnp.exp(-jnp.abs(x)))


def _dot(a, b):
    return jnp.dot(a, b, preferred_element_type=F32)


def _whole(memory_space=pltpu.VMEM):
    return pl.BlockSpec(memory_space=memory_space)


def _lane_chunk(c):
    return slice(c * LANES, (c + 1) * LANES)


def _subtiles(ref):
    bb, tt = ref.shape[0], ref.shape[1]
    return [(b, slice(s * SUB_T, (s + 1) * SUB_T)) for b in range(bb) for s in range(tt // SUB_T)]


def _token_block(B, T, rows, per_seq_m):
    tt = min(T, rows)
    bb = 1 if per_seq_m else min(B, rows // tt)
    return bb, tt


def _mod_kernel(cond_ref, w_ref, b_ref, o_ref):
    c = cond_ref[...]
    s = (c * _sigmoid(c)).astype(BF16)
    o_ref[...] = _dot(s, w_ref[...].astype(BF16)) + b_ref[...]


def _modulation(cond, w_mod, b_mod):
    rows = cond.shape[0]
    n = w_mod.shape[1]
    bn = D
    return pl.pallas_call(
        _mod_kernel,
        grid=(n // bn,),
        in_specs=[
            pl.BlockSpec((rows, D), lambda j: (0, 0)),
            pl.BlockSpec((D, bn), lambda j: (0, j)),
            pl.BlockSpec((1, bn), lambda j: (0, j)),
        ],
        out_specs=pl.BlockSpec((rows, bn), lambda j: (0, j)),
        out_shape=jax.ShapeDtypeStruct((rows, n), F32),
        compiler_params=pltpu.CompilerParams(dimension_semantics=("arbitrary",)),
        name="modulation",
    )(cond, w_mod, b_mod.reshape(1, n))


def _stage1_kernel(has_pe, *refs):
    if has_pe:
        x_ref, per_ref, pec_ref, refs = refs[0], refs[1], refs[2], refs[3:]
    else:
        x_ref, refs = refs[0], refs[1:]
    (m_ref, n1_ref, n2_ref, gv_ref, wg_ref, wu_ref, wd_ref, win_ref,
     x1_ref, zx_ref, za_ref) = refs
    m = m_ref[0]
    subs = _subtiles(x_ref)

    xs, hs = [], []
    for b, rs in subs:
        x = x_ref[b, rs, :]
        if has_pe:
            reps = GRID_W // SUBLANES
            r0 = rs.start // GRID_W
            row_half = jnp.concatenate(
                [jnp.tile(per_ref[r * SUBLANES:(r + 1) * SUBLANES, :], (reps, 1))
                 for r in range(r0, r0 + SUB_T // GRID_W)], axis=0)
            col_half = jnp.tile(pec_ref[...], (SUB_T // GRID_W, 1))
            x = x + jnp.concatenate([row_half, col_half], axis=1)
        xs.append(x)
        hs.append(_modulate(x, n1_ref[...], m[0:1], m[1:2]).astype(BF16))

    h2s = []
    for (b, rs), x, h in zip(subs, xs, hs):
        g = _dot(h, wg_ref[...])
        u = _dot(h, wu_ref[...])
        act = (_silu_of_half(g) * u).astype(BF16)
        x1 = x + (0.5 * m[2:3]) * _dot(act, wd_ref[:, 0:D])
        x1_ref[b, rs, :] = x1
        h2s.append(_modulate(x1, n2_ref[...], m[3:4], m[4:5]).astype(BF16))

    branches = ((2, _gelu_of_half), (3, lambda v: _rms(_gelu_of_half(v), gv_ref[...])),
                (4, _sigmoid_of_half), (5, _sigmoid_of_half), (1, _gelu_of_half))
    for (b, rs), h2 in zip(subs, h2s):
        for j, (src, fn) in enumerate(branches):
            z = _dot(h2, win_ref[:, src * D:(src + 1) * D])
            za_ref[b, rs, j * D:(j + 1) * D] = fn(z).astype(BF16)
        zx_ref[b, rs, :] = _dot(h2, win_ref[:, 0:D])


def _stage1(x, pe_tables, m, n1, n2, gv, wg, wu, wd, win):
    B, T, _ = x.shape
    has_pe = pe_tables is not None
    per_batch_m = m.shape[0] > 1
    bb, tt = _token_block(B, T, STAGE1_T, per_batch_m)
    m_map = (lambda t, b: (b, 0, 0)) if per_batch_m else (lambda t, b: (0, 0, 0))
    in_specs = [pl.BlockSpec((bb, tt, D), lambda t, b: (b, t, 0))]
    args = [x]
    if has_pe:
        pe_rows, pe_cols = pe_tables
        in_specs += [pl.BlockSpec((tt // GRID_W * SUBLANES, D // 2), lambda t, b: (t, 0)), _whole()]
        args += [pe_rows, pe_cols]
    in_specs += [pl.BlockSpec((1, N_MOD, D), m_map)] + [_whole()] * 7
    args += [m, n1, n2, gv, wg, wu, wd, win]
    blk = lambda w: pl.BlockSpec((bb, tt, w), lambda t, b: (b, t, 0))
    return pl.pallas_call(
        functools.partial(_stage1_kernel, has_pe),
        grid=(T // tt, B // bb),
        in_specs=in_specs,
        out_specs=[blk(D), blk(D), blk(Z_ACT)],
        out_shape=[jax.ShapeDtypeStruct((B, T, D), F32),
                   jax.ShapeDtypeStruct((B, T, D), F32),
                   jax.ShapeDtypeStruct((B, T, Z_ACT), BF16)],
        compiler_params=pltpu.CompilerParams(
            dimension_semantics=("arbitrary", "arbitrary"),
            vmem_limit_bytes=VMEM_LIMIT),
        name="stage1_ffn_inproj",
    )(*args)


def _gates(blk, xr, wg_ref, bg_ref, lam_ref):
    lo = blk * GATE_BLOCK
    c4 = (0.5 * LRU_C) * _log_sigmoid(lam_ref[:, lo:lo + GATE_BLOCK])
    t = jnp.tanh(_dot(xr.astype(BF16), wg_ref[blk]) + bg_ref[blk])
    t_r = t[:, 0:GATE_BLOCK]
    t_i = t[:, GATE_BLOCK:2 * GATE_BLOCK]
    log_a = t_r * c4 + c4
    a = jnp.exp(log_a)
    q = (-0.25 * jnp.tanh(log_a)) * (a * a + 1.0)
    half_mult = jnp.where(q > 0.0, q * lax.rsqrt(q), 0.0)
    w = half_mult * xr.astype(F32)
    return a, w * t_i + w


def _init_carry(first, h0_ref, carry_scr):
    @pl.when(first)
    def _():
        for c in range(N_LANE_CHUNKS):
            carry_scr[c] = h0_ref[:, _lane_chunk(c)]


def _recurrence_block(blk, reverse, tt, a, bx, emit, carry_scr):
    nb = SCAN_BATCH
    for half in range(GATE_HALVES):
        c = blk * GATE_HALVES + half
        cs = _lane_chunk(half)
        h = carry_scr[c]
        for t in (range(tt - 1, -1, -1) if reverse else range(tt)):
            rs = slice(t * nb, (t + 1) * nb)
            h = a[rs, cs] * h + bx[rs, cs]
            emit(rs, c, h)
        carry_scr[c] = h


def _scan_bwd_kernel(zc_ref, zl_ref, zrt_ref, h0_ref, cw_ref, cb_ref, wg_ref, bg_ref, lam_ref,
                     h_ref, xr_ref, ext_scr, carry_scr):
    nb, tt, _ = zc_ref.shape
    j = pl.program_id(1)
    nt = pl.num_programs(1)
    ti = nt - 1 - j
    rows = tt * nb
    lead = CONV_LEFT * nb

    _init_carry(j == 0, h0_ref, carry_scr)

    def emit(rows, c, h):
        h_ref[0, rows, _lane_chunk(c)] = h

    for c in range(N_LANE_CHUNKS):
        cols = _lane_chunk(c)
        for k in range(CONV_LEFT):
            ext_scr[c, k * nb:(k + 1) * nb, :] = jnp.where(
                ti > 0, zl_ref[:, SUBLANES - CONV_LEFT + k, cols], 0.0)
        for k in range(CONV_RIGHT):
            ext_scr[c, lead + rows + k * nb:lead + rows + (k + 1) * nb, :] = jnp.where(
                ti < nt - 1, zrt_ref[:, k, cols], 0.0)
        for s in range(nb):
            ext_scr[c, pl.ds(lead + s, tt, stride=nb), :] = zc_ref[s, :, cols]

    for blk in range(N_GATE_BLOCKS):
        parts = []
        for half in range(GATE_HALVES):
            c = blk * GATE_HALVES + half
            cols = _lane_chunk(c)
            acc = ext_scr[c, 0:rows, :] * cw_ref[0:1, cols] + cb_ref[:, cols]
            for k in range(1, CONV_W):
                acc = acc + ext_scr[c, k * nb:k * nb + rows, :] * cw_ref[k:k + 1, cols]
            parts.append(acc)
        xr = jnp.concatenate(parts, axis=1)
        xr_ref[0, :, blk * GATE_BLOCK:(blk + 1) * GATE_BLOCK] = xr.astype(xr_ref.dtype)
        a, bx = _gates(blk, xr, wg_ref, bg_ref, lam_ref)
        _recurrence_block(blk, True, tt, a, bx, emit, carry_scr)


def _scan_fwd_kernel(xr_ref, h0_ref, hb_ref, wg_ref, bg_ref, lam_ref,
                     hsum_ref, hlast_ref, o_scr, carry_scr):
    nb, tt, _ = hsum_ref.shape
    _init_carry(pl.program_id(1) == 0, h0_ref, carry_scr)

    def emit(rows, c, hf):
        o_scr[c, rows, :] = hf + hb_ref[0, rows, _lane_chunk(c)]

    for blk in range(N_GATE_BLOCKS):
        xr = xr_ref[0, :, blk * GATE_BLOCK:(blk + 1) * GATE_BLOCK]
        a, bx = _gates(blk, xr, wg_ref, bg_ref, lam_ref)
        _recurrence_block(blk, False, tt, a, bx, emit, carry_scr)
    for c in range(N_LANE_CHUNKS):
        hlast_ref[:, _lane_chunk(c)] = carry_scr[c]
        for s in range(nb):
            hsum_ref[s, :, _lane_chunk(c)] = o_scr[c, pl.ds(s, tt, stride=nb), :].astype(hsum_ref.dtype)


def _scan_scratch(nb, tt, n_tiles):
    tile = pltpu.VMEM((N_LANE_CHUNKS, tt * nb, LANES), F32)
    return [tile] * n_tiles + [pltpu.VMEM((N_LANE_CHUNKS, nb, LANES), F32)]


def _scan_bwd(zx, h0, conv_w, conv_b, wgate, bgate, lam):
    B, T, _ = zx.shape
    nb = SCAN_BATCH
    tt = min(T, SCAN_T)
    nt = T // tt
    tpb = tt // SUBLANES
    nhb = T // SUBLANES
    ti = lambda j: nt - 1 - j
    return pl.pallas_call(
        _scan_bwd_kernel,
        grid=(B // nb, nt),
        in_specs=[
            pl.BlockSpec((nb, tt, D), lambda g, j: (g, ti(j), 0)),
            pl.BlockSpec((nb, SUBLANES, D),
                         lambda g, j: (g, jnp.maximum(ti(j) * tpb - 1, 0), 0)),
            pl.BlockSpec((nb, SUBLANES, D),
                         lambda g, j: (g, jnp.minimum((ti(j) + 1) * tpb, nhb - 1), 0)),
            pl.BlockSpec((nb, D), lambda g, j: (g, 0)),
            _whole(), _whole(), _whole(), _whole(), _whole(),
        ],
        out_specs=[pl.BlockSpec((1, tt * nb, D), lambda g, j: (g, ti(j), 0))] * 2,
        out_shape=[jax.ShapeDtypeStruct((B // nb, T * nb, D), F32),
                   jax.ShapeDtypeStruct((B // nb, T * nb, D), BF16)],
        scratch_shapes=[pltpu.VMEM((N_LANE_CHUNKS, (tt + CONV_W - 1) * nb, LANES), F32)]
        + _scan_scratch(nb, tt, 0),
        compiler_params=pltpu.CompilerParams(
            dimension_semantics=("arbitrary", "arbitrary"),
            vmem_limit_bytes=VMEM_LIMIT),
        name="scan_bwd",
    )(zx, zx, zx, h0, conv_w, conv_b, wgate, bgate, lam)


def _scan_fwd(xr_tm, hb_tm, h0, wgate, bgate, lam):
    G, rows_total, _ = xr_tm.shape
    nb = SCAN_BATCH
    T = rows_total // nb
    tt = min(T, SCAN_T)
    tm_tile = pl.BlockSpec((1, tt * nb, D), lambda g, j: (g, j, 0))
    return pl.pallas_call(
        _scan_fwd_kernel,
        grid=(G, T // tt),
        in_specs=[
            tm_tile,
            pl.BlockSpec((nb, D), lambda g, j: (g, 0)),
            tm_tile,
            _whole(), _whole(), _whole(),
        ],
        out_specs=[pl.BlockSpec((nb, tt, D), lambda g, j: (g, j, 0)),
                   pl.BlockSpec((nb, D), lambda g, j: (g, 0))],
        out_shape=[jax.ShapeDtypeStruct((G * nb, T, D), BF16),
                   jax.ShapeDtypeStruct((G * nb, D), F32)],
        scratch_shapes=_scan_scratch(nb, tt, 1),
        compiler_params=pltpu.CompilerParams(
            dimension_semantics=("arbitrary", "arbitrary"),
            vmem_limit_bytes=VMEM_LIMIT),
        name="scan_fwd",
    )(xr_tm, h0, hb_tm, wgate, bgate, lam)


def _stage3_kernel(x1_ref, hsum_ref, za_ref, m_ref, ws_ref, bs_ref,
                   wbr_ref, wbg_ref, wo_ref, n3_ref, wg_ref, wu_ref, wd_ref, nf_ref,
                   y_ref, mixed_scr):
    m = m_ref[0]
    subs = _subtiles(x1_ref)

    x2s, hs = [], []
    for i, (b, rs) in enumerate(subs):
        zcol = lambda c: za_ref[b, rs, c * D:(c + 1) * D]
        y_rnn = zcol(ZA_RNN_GATE).astype(F32) * hsum_ref[b, rs, :].astype(F32)
        rnn_proj = _dot(y_rnn.astype(BF16), wbr_ref[:, 0:D])
        n_chunks = SUB_T // CHUNK
        for g in range(GROUPS):
            cs = _lane_chunk(g)
            vn_g = jnp.concatenate(
                [za_ref[b, rs.start + n * CHUNK:rs.start + (n + 1) * CHUNK, D + g * LANES:D + (g + 1) * LANES]
                 for n in range(n_chunks)], axis=1)
            mix_g = _dot(ws_ref[g], vn_g)
            for n in range(n_chunks):
                mixed_scr[i, n * CHUNK:(n + 1) * CHUNK, cs] = mix_g[:, _lane_chunk(n)] + bs_ref[:, cs]
        y_g = zcol(0).astype(F32) * mixed_scr[i]
        g_proj = _dot(y_g.astype(BF16), wbg_ref[:, 0:D])
        merged = zcol(2).astype(F32) * rnn_proj + zcol(3).astype(F32) * g_proj
        x2 = x1_ref[b, rs, :] + m[5:6] * _dot(merged.astype(BF16), wo_ref[:, 0:D])
        x2s.append(x2)
        hs.append(_modulate(x2, n3_ref[...], m[6:7], m[7:8]).astype(BF16))

    for (b, rs), x2, h in zip(subs, x2s, hs):
        g = _dot(h, wg_ref[...])
        uu = _dot(h, wu_ref[...])
        act = (_silu_of_half(g) * uu).astype(BF16)
        x3 = x2 + (0.5 * m[8:9]) * _dot(act, wd_ref[:, 0:D])
        y_ref[b, rs, :] = _rms(x3, nf_ref[...])


def _stage3(x1, h_sum, za, m, ws, bs, wbr, wbg, wo, n3, wg, wu, wd, nf):
    B, T, _ = x1.shape
    per_batch_m = m.shape[0] > 1
    bb, tt = _token_block(B, T, STAGE3_T, per_batch_m)
    m_map = (lambda t, b: (b, 0, 0)) if per_batch_m else (lambda t, b: (0, 0, 0))
    blk = lambda w: pl.BlockSpec((bb, tt, w), lambda t, b: (b, t, 0))
    return pl.pallas_call(
        _stage3_kernel,
        grid=(T // tt, B // bb),
        in_specs=[blk(D), blk(D), blk(Z_ACT),
                  pl.BlockSpec((1, N_MOD, D), m_map)] + [_whole()] * 10,
        out_specs=blk(D),
        out_shape=jax.ShapeDtypeStruct((B, T, D), F32),
        scratch_shapes=[pltpu.VMEM((bb * tt // SUB_T, SUB_T, D), F32)],
        compiler_params=pltpu.CompilerParams(
            dimension_semantics=("arbitrary", "arbitrary"),
            vmem_limit_bytes=VMEM_LIMIT),
        name="stage3_mix_ffn",
    )(x1, h_sum, za, m, ws, bs, wbr, wbg, wo, n3, wg, wu, wd, nf)


def _grid_pos_tables(n_tokens):
    q = D // 4
    freqs = 1.0 / (10000.0 ** (jnp.arange(q, dtype=F32) / q))
    half = lambda idx: jnp.concatenate(
        [jnp.sin(idx[:, None] * freqs), jnp.cos(idx[:, None] * freqs)], axis=-1)
    rows = half(jnp.arange(n_tokens // GRID_W).astype(F32))
    cols = half(jnp.arange(GRID_W).astype(F32))
    return jnp.repeat(rows, SUBLANES, axis=0), cols


def _gate_weights(w_r, w_i, b_r, b_i):
    def bd(w):
        w4 = w.reshape(N_GATE_BLOCKS, HEADS_PER_BLOCK, HEAD_DIM, HEAD_DIM)
        eye = jnp.eye(HEADS_PER_BLOCK, dtype=w.dtype)
        return jnp.einsum('ghij,hk->ghikj', w4, eye).reshape(N_GATE_BLOCKS, GATE_BLOCK, GATE_BLOCK)
    wg = (0.5 * jnp.concatenate([bd(w_r), bd(w_i)], axis=-1)).astype(BF16)
    bg = 0.5 * jnp.concatenate([b_r.reshape(N_GATE_BLOCKS, 1, GATE_BLOCK),
                                b_i.reshape(N_GATE_BLOCKS, 1, GATE_BLOCK)], axis=-1)
    return wg, bg


def kernel(x_prompt, x_sample, state_rnn_fwd, state_rnn_bwd, c, c_ctx, w_mod, b_mod, norm1, norm2, norm3, ff1_gate, ff1_up, ff1_down, w_in, conv_w, conv_b, w_r, b_r, w_i, b_i, lam, gmlp_norm, w_s, b_s, w_br, w_bg, w_out, ff2_gate, ff2_up, ff2_down, norm_f):
    l = 0
    nbatch_lat = c.shape[0]
    cond_rows = 2 * SUBLANES
    cond = jnp.zeros((cond_rows, D), F32).at[:nbatch_lat].set(c).at[nbatch_lat].set(c_ctx)
    m_all = _modulation(cond, w_mod[l], b_mod[l])
    m_lat = m_all[:nbatch_lat].reshape(nbatch_lat, N_MOD, D)
    m_ctx = m_all[nbatch_lat:nbatch_lat + 1].reshape(1, N_MOD, D)

    row = lambda v: v.reshape(1, D)
    bf = lambda w: w.astype(BF16)

    def bfp(w):
        assert (w.shape[-1] // LANES) % 2 == 0
        return jnp.pad(w.astype(BF16), ((0, 0), (0, LANES)))
    n1, n2, n3, nf, gv = row(norm1[l]), row(norm2[l]), row(norm3[l]), row(norm_f), row(gmlp_norm[l])
    wg1, wu1, wd1 = bf(0.5 * ff1_gate[l]), bf(ff1_up[l]), bfp(ff1_down[l])
    wg2, wu2, wd2 = bf(0.5 * ff2_gate[l]), bf(ff2_up[l]), bfp(ff2_down[l])
    in_scale = jnp.concatenate([jnp.ones((D,), F32), jnp.full((N_BRANCH * D,), 0.5, F32)])
    win, wbr, wbg, wo = bfp(w_in[l] * in_scale), bfp(w_br[l]), bfp(w_bg[l]), bfp(w_out[l])
    ws = bf(w_s[l])
    bs = jnp.repeat(b_s[l].T, D // GROUPS, axis=1)
    cw, cb, lam_l = conv_w[l], row(conv_b[l]), lam[l]
    (wgt_f, bgt_f), (wgt_b, bgt_b) = [
        _gate_weights(w_r[l, d], w_i[l, d], b_r[l, d], b_i[l, d]) for d in range(2)]

    def trunk(x, pe_tables, m, h0_f, h0_b):
        x1, zx, za = _stage1(x, pe_tables, m, n1, n2, gv, wg1, wu1, wd1, win)
        hb_tm, xr_tm = _scan_bwd(zx, h0_b, cw, cb, wgt_b, bgt_b, row(lam_l[1]))
        h_sum, hf_last = _scan_fwd(xr_tm, hb_tm, h0_f, wgt_f, bgt_f, row(lam_l[0]))
        y = _stage3(x1, h_sum, za, m, ws, bs, wbr, wbg, wo, n3, wg2, wu2, wd2, nf)
        return y, hf_last, hb_tm[:, :SCAN_BATCH, :].reshape(-1, 1, D)

    zeros = jnp.zeros((x_prompt.shape[0], D), F32)
    y_prompt, hf_last, new_b = trunk(x_prompt, None, m_ctx, zeros, zeros)
    new_f = hf_last[:, None, :]

    y_sample, _, _ = trunk(x_sample, _grid_pos_tables(x_sample.shape[1]), m_lat,
                           state_rnn_fwd[:, l], state_rnn_bwd[:, l])
    return (y_prompt, y_sample, new_f, new_b)
```
